```python
import math
import jax, jax.numpy as jnp
from jax import lax
import numpy as np

D_MODEL = 1024
BATCH = 8
SEQ = 8192
DEPTH = 1

N_META = 16
D_SSM = D_MODEL // 2
SSM_GROUP = 16
N_SSM_GROUPS = D_SSM // SSM_GROUP
SSM_STATE = 64
D_CONV = D_MODEL // 2
CONV_WIDTH = 31
N_EXPERTS = 32
TOP_K = 4
D_EXPERT = D_MODEL
SWIGLU_LIMIT = 7.0
SWIGLU_ALPHA = 1.702
RMS_EPS = 1e-6
LN_EPS = 1e-5
DT_MIN = 1e-3
DT_MAX = 1e-1
D_IN_PROJ = D_SSM + 2 * D_CONV + 2 * D_MODEL

kernel_name = "hybrid_s5_conformer_moe_block"


def _rmsnorm(x, g):
    xf = x.astype(jnp.float32)
    y = xf * lax.rsqrt(jnp.mean(xf * xf, axis=-1, keepdims=True) + RMS_EPS)
    return (y * g.astype(jnp.float32)).astype(x.dtype)


def _layernorm(x, g, b):
    xf = x.astype(jnp.float32)
    mu = jnp.mean(xf, axis=-1, keepdims=True)
    var = jnp.mean(jnp.square(xf - mu), axis=-1, keepdims=True)
    y = (xf - mu) * lax.rsqrt(var + LN_EPS)
    return (y * g.astype(jnp.float32) + b.astype(jnp.float32)).astype(x.dtype)


def _ssm_combine(left, right):
    ar_i, ai_i, br_i, bi_i = left
    ar_j, ai_j, br_j, bi_j = right
    return (ar_j * ar_i - ai_j * ai_i,
            ar_j * ai_i + ai_j * ar_i,
            ar_j * br_i - ai_j * bi_i + br_j,
            ar_j * bi_i + ai_j * br_i + bi_j)


def _s5_branch(u, a_re, a_im, log_dt, b_re, b_im, c_re, c_im, d_skip):
    bsz, seq_len, _ = u.shape
    uf = u.astype(jnp.float32).reshape(bsz, seq_len, N_SSM_GROUPS, SSM_GROUP)
    a_re = a_re.astype(jnp.float32)
    a_im = a_im.astype(jnp.float32)
    dt = jnp.exp(log_dt.astype(jnp.float32))[:, None]
    decay = jnp.exp(a_re * dt)
    abar_re = decay * jnp.cos(a_im * dt)
    abar_im = decay * jnp.sin(a_im * dt)
    den = a_re * a_re + a_im * a_im
    nr = abar_re - 1.0
    coef_re = (nr * a_re + abar_im * a_im) / den
    coef_im = (abar_im * a_re - nr * a_im) / den
    bu_re = jnp.einsum('blgh,gph->lbgp', uf, b_re.astype(jnp.float32))
    bu_im = jnp.einsum('blgh,gph->lbgp', uf, b_im.astype(jnp.float32))
    drive_re = coef_re * bu_re - coef_im * bu_im
    drive_im = coef_re * bu_im + coef_im * bu_re
    ar = jnp.broadcast_to(abar_re, (seq_len, 1, N_SSM_GROUPS, SSM_STATE))
    ai = jnp.broadcast_to(abar_im, (seq_len, 1, N_SSM_GROUPS, SSM_STATE))
    _, _, s_re, s_im = lax.associative_scan(_ssm_combine, (ar, ai, drive_re, drive_im), axis=0)
    y = (jnp.einsum('lbgp,ghp->blgh', s_re, c_re.astype(jnp.float32))
         - jnp.einsum('lbgp,ghp->blgh', s_im, c_im.astype(jnp.float32)))
    y = y.reshape(bsz, seq_len, D_SSM) + d_skip.astype(jnp.float32) * u.astype(jnp.float32)
    return y.astype(u.dtype)


def _conformer_branch(conv_in, dw_w, dw_b, ln_g, ln_b, w_conv_out):
    val, gate = jnp.split(conv_in, 2, axis=-1)
    c = val * jax.nn.sigmoid(gate)
    c = lax.conv_general_dilated(
        c, dw_w.astype(c.dtype), window_strides=(1,), padding=[(CONV_WIDTH - 1, 0)],
        dimension_numbers=('NWC', 'WIO', 'NWC'), feature_group_count=D_CONV) + dw_b.astype(c.dtype)
    c = _layernorm(c, ln_g, ln_b)
    c = jax.nn.silu(c)
    return c @ w_conv_out


def _hybrid_mixer(h, w_in, a_re, a_im, log_dt, b_re, b_im, c_re, c_im, d_skip,
                  w_s5_lin, w_s5_gate, dw_w, dw_b, ln_g, ln_b, w_conv_out, w_out):
    z = h @ w_in
    s1 = D_SSM
    s2 = s1 + 2 * D_CONV
    s3 = s2 + D_MODEL
    u_s5, conv_in, gate_a, gate_b = z[..., :s1], z[..., s1:s2], z[..., s2:s3], z[..., s3:]
    ya = jax.nn.gelu(_s5_branch(u_s5, a_re, a_im, log_dt, b_re, b_im, c_re, c_im, d_skip), approximate=False)
    ya = (ya @ w_s5_lin) * jax.nn.sigmoid(ya @ w_s5_gate)
    yb = _conformer_branch(conv_in, dw_w, dw_b, ln_g, ln_b, w_conv_out)
    merged = jax.nn.sigmoid(gate_a) * ya + jax.nn.sigmoid(gate_b) * yb
    return merged @ w_out


def _moe(h, router_w, router_b, w_gate_up, b_gate_up, w_down, b_down):
    bsz, seq_len, d = h.shape
    hf = h.reshape(bsz * seq_len, d)
    logits = (hf @ router_w + router_b).astype(jnp.float32)
    top_val, top_idx = lax.top_k(logits, TOP_K)
    probs = jax.nn.softmax(top_val, axis=-1)
    combine = jnp.sum(jax.nn.one_hot(top_idx, N_EXPERTS, dtype=jnp.float32) * probs[..., None], axis=1)
    combine = combine.astype(h.dtype)
    out = jnp.zeros_like(hf)
    for e in range(N_EXPERTS):
        gu = hf @ w_gate_up[e] + b_gate_up[e]
        g, lin = gu[:, :D_EXPERT], gu[:, D_EXPERT:]
        g = jnp.minimum(g, SWIGLU_LIMIT)
        lin = jnp.clip(lin, -SWIGLU_LIMIT, SWIGLU_LIMIT)
        act = g * jax.nn.sigmoid(SWIGLU_ALPHA * g) * (lin + 1.0)
        out = out + combine[:, e:e + 1] * (act @ w_down[e] + b_down[e])
    return out.reshape(bsz, seq_len, d)


def setup_inputs(seed: int = 0) -> dict:
    key = jax.random.key(seed)
    ks = jax.random.split(key, 32)
    f32 = jnp.float32
    nrm = lambda k, shape, scale: scale * jax.random.normal(k, shape, f32)
    G, P, H = N_SSM_GROUPS, SSM_STATE, SSM_GROUP
    a_im = jnp.broadcast_to(math.pi * jnp.arange(P, dtype=f32), (DEPTH, G, P))
    return {
        "x": jax.random.normal(ks[0], (BATCH, SEQ, D_MODEL), f32),
        "meta_tokens": nrm(ks[1], (N_META, D_MODEL), 1.0),
        "norm_mix_g": 1.0 + nrm(ks[2], (DEPTH, D_MODEL), 0.02),
        "w_in": nrm(ks[3], (DEPTH, D_MODEL, D_IN_PROJ), D_MODEL ** -0.5),
        "ssm_a_re": -0.5 + nrm(ks[4], (DEPTH, G, P), 0.01),
        "ssm_a_im": a_im + nrm(ks[5], (DEPTH, G, P), 0.01),
        "ssm_log_dt": jax.random.uniform(ks[6], (DEPTH, G), f32, math.log(DT_MIN), math.log(DT_MAX)),
        "ssm_b_re": nrm(ks[7], (DEPTH, G, P, H), (2.0 * H) ** -0.5),
        "ssm_b_im": nrm(ks[8], (DEPTH, G, P, H), (2.0 * H) ** -0.5),
        "ssm_c_re": nrm(ks[9], (DEPTH, G, H, P), (2.0 * P) ** -0.5),
        "ssm_c_im": nrm(ks[10], (DEPTH, G, H, P), (2.0 * P) ** -0.5),
        "ssm_d": nrm(ks[11], (DEPTH, D_SSM), 1.0),
        "w_s5_lin": nrm(ks[12], (DEPTH, D_SSM, D_MODEL), D_SSM ** -0.5),
        "w_s5_gate": nrm(ks[13], (DEPTH, D_SSM, D_MODEL), D_SSM ** -0.5),
        "conv_dw_w": nrm(ks[14], (DEPTH, CONV_WIDTH, 1, D_CONV), CONV_WIDTH ** -0.5),
        "conv_dw_b": nrm(ks[15], (DEPTH, D_CONV), 0.02),
        "conv_ln_g": 1.0 + nrm(ks[16], (DEPTH, D_CONV), 0.02),
        "conv_ln_b": nrm(ks[17], (DEPTH, D_CONV), 0.02),
        "w_conv_out": nrm(ks[18], (DEPTH, D_CONV, D_MODEL), D_CONV ** -0.5),
        "w_out": nrm(ks[19], (DEPTH, D_MODEL, D_MODEL), D_MODEL ** -0.5),
        "norm_ffn_g": 1.0 + nrm(ks[20], (DEPTH, D_MODEL), 0.02),
        "router_w": nrm(ks[21], (DEPTH, D_MODEL, N_EXPERTS), D_MODEL ** -0.5),
        "router_b": nrm(ks[22], (DEPTH, N_EXPERTS), 0.01),
        "w_gate_up": nrm(ks[23], (DEPTH, N_EXPERTS, D_MODEL, 2 * D_EXPERT), D_MODEL ** -0.5),
        "b_gate_up": nrm(ks[24], (DEPTH, N_EXPERTS, 2 * D_EXPERT), 0.02),
        "w_down": nrm(ks[25], (DEPTH, N_EXPERTS, D_EXPERT, D_MODEL), D_EXPERT ** -0.5),
        "b_down": nrm(ks[26], (DEPTH, N_EXPERTS, D_MODEL), 0.02),
        "norm_final_g": 1.0 + nrm(ks[27], (D_MODEL,), 0.02),
    }


def reference(x, meta_tokens, norm_mix_g, w_in, ssm_a_re, ssm_a_im, ssm_log_dt, ssm_b_re, ssm_b_im,
              ssm_c_re, ssm_c_im, ssm_d, w_s5_lin, w_s5_gate, conv_dw_w, conv_dw_b, conv_ln_g, conv_ln_b,
              w_conv_out, w_out, norm_ffn_g, router_w, router_b, w_gate_up, b_gate_up, w_down, b_down,
              norm_final_g):
    bsz = x.shape[0]
    meta = jnp.broadcast_to(meta_tokens[None].astype(x.dtype), (bsz, N_META, D_MODEL))
    h = jnp.concatenate([meta, x], axis=1)
    for i in range(DEPTH):
        h = h + _hybrid_mixer(_rmsnorm(h, norm_mix_g[i]), w_in[i], ssm_a_re[i], ssm_a_im[i], ssm_log_dt[i],
                              ssm_b_re[i], ssm_b_im[i], ssm_c_re[i], ssm_c_im[i], ssm_d[i],
                              w_s5_lin[i], w_s5_gate[i], conv_dw_w[i], conv_dw_b[i], conv_ln_g[i], conv_ln_b[i],
                              w_conv_out[i], w_out[i])
        h = h + _moe(_rmsnorm(h, norm_ffn_g[i]), router_w[i], router_b[i], w_gate_up[i], b_gate_up[i],
                     w_down[i], b_down[i])
    return _rmsnorm(h, norm_final_g)[:, N_META:]
```

```python
import functools
import math

import jax
import jax.numpy as jnp
from jax import lax
from jax.experimental import pallas as pl
from jax.experimental.pallas import tpu as pltpu

RMS_EPS = 1e-6
LN_EPS = 1e-5
TOP_K = 4
SWIGLU_LIMIT = 7.0
SWIGLU_ALPHA = 1.702

LANES = 128
SUBLANES = 8
VMEM_LIMIT = 56 * 1024 * 1024


def _sigmoid(v):
    return 1.0 / (1.0 + jnp.exp(-v))


def _const_spec(shape):
    nd = len(shape)
    return pl.BlockSpec(shape, lambda *_: (0,) * nd)


def _mixer_kernel(x_ref, meta_ref, g_ref, win_ref, ar_ref, ai_ref, bd_ref, ct_ref, dskip_ref,
                  wlin_ref, wgate_ref, dww_ref, dwb_ref, lng_ref, lnb_ref, wco_ref, wout_ref,
                  out_ref,
                  xs_ref, st_ref, state_ref, cbuf_ref, conv_ref,
                  *, t_chunk, d_ssm, d_conv, conv_w, sb_lanes):
    i = pl.program_id(0)
    bsz = SUBLANES
    rows = t_chunk * bsz
    d_model = xs_ref.shape[1]
    n_sb = d_ssm // LANES
    hist = (conv_w - 1) * bsz

    @pl.when(i == 0)
    def _():
        state_ref[...] = jnp.zeros_like(state_ref)
        cbuf_ref[...] = jnp.zeros_like(cbuf_ref)
        for t in range(t_chunk):
            xs_ref[t * bsz:(t + 1) * bsz, :] = jnp.broadcast_to(meta_ref[t:t + 1, :], (bsz, d_model))

    @pl.when(i > 0)
    def _():
        for t in range(t_chunk):
            xs_ref[t * bsz:(t + 1) * bsz, :] = x_ref[:, pl.ds(t, 1), :].reshape(bsz, d_model)

    xs = xs_ref[...]
    hn = xs * lax.rsqrt(jnp.mean(xs * xs, axis=-1, keepdims=True) + RMS_EPS) * g_ref[...]
    z = jnp.dot(hn.astype(jnp.bfloat16), win_ref[...], preferred_element_type=jnp.float32)
    s1 = d_ssm
    s2 = s1 + 2 * d_conv
    s3 = s2 + d_model

    u = z[:, :s1]
    u_bf = u.astype(jnp.bfloat16)
    for sb in range(n_sb):
        st_ref[:, sb * 2 * sb_lanes:(sb + 1) * 2 * sb_lanes] = jnp.dot(
            u_bf[:, sb * LANES:(sb + 1) * LANES], bd_ref[sb], preferred_element_type=jnp.float32)
    n_lb = sb_lanes // LANES
    for sb in range(n_sb):
        base = sb * 2 * sb_lanes
        for j in range(n_lb):
            lre = slice(base + j * LANES, base + (j + 1) * LANES)
            lim = slice(base + sb_lanes + j * LANES, base + sb_lanes + (j + 1) * LANES)
            la = slice(sb * sb_lanes + j * LANES, sb * sb_lanes + (j + 1) * LANES)
            ar = jnp.broadcast_to(ar_ref[:, la], (bsz, LANES))
            ai = jnp.broadcast_to(ai_ref[:, la], (bsz, LANES))
            sre = state_ref[:, lre]
            sim = state_ref[:, lim]
            for t in range(t_chunk):
                r = slice(t * bsz, (t + 1) * bsz)
                nre = ar * sre - ai * sim + st_ref[r, lre]
                nim = ar * sim + ai * sre + st_ref[r, lim]
                st_ref[r, lre] = nre
                st_ref[r, lim] = nim
                sre, sim = nre, nim
            state_ref[:, lre] = sre
            state_ref[:, lim] = sim
    ys = []
    for sb in range(n_sb):
        ys.append(jnp.dot(st_ref[:, sb * 2 * sb_lanes:(sb + 1) * 2 * sb_lanes].astype(jnp.bfloat16),
                          ct_ref[sb], preferred_element_type=jnp.float32))
    y = jnp.concatenate(ys, axis=-1) if n_sb > 1 else ys[0]
    y = y + dskip_ref[...] * u
    ya = 0.5 * y * (1.0 + lax.erf(y * (1.0 / math.sqrt(2.0))))
    ya_bf = ya.astype(jnp.bfloat16)
    ya2 = (jnp.dot(ya_bf, wlin_ref[...], preferred_element_type=jnp.float32)
           * _sigmoid(jnp.dot(ya_bf, wgate_ref[...], preferred_element_type=jnp.float32)))

    cbuf_ref[hist:hist + rows, :] = z[:, s1:s1 + d_conv] * _sigmoid(z[:, s1 + d_conv:s2])
    for lb in range(d_conv // LANES):
        ls = slice(lb * LANES, (lb + 1) * LANES)
        wk = [jnp.broadcast_to(dww_ref[k:k + 1, ls], (bsz, LANES)) for k in range(conv_w)]

        def conv_rows(r, carry, ls=ls, wk=wk):
            r0 = pl.multiple_of(r * bsz, bsz)
            acc = wk[0] * cbuf_ref[pl.ds(r0, bsz), ls]
            for k in range(1, conv_w):
                acc = acc + wk[k] * cbuf_ref[pl.ds(r0 + k * bsz, bsz), ls]
            conv_ref[pl.ds(r0, bsz), ls] = acc
            return carry

        lax.fori_loop(0, t_chunk, conv_rows, 0)
    cbuf_ref[0:hist, :] = cbuf_ref[rows:rows + hist, :]
    c = conv_ref[...] + dwb_ref[...]
    mu = jnp.mean(c, axis=-1, keepdims=True)
    cc = c - mu
    var = jnp.mean(cc * cc, axis=-1, keepdims=True)
    c = cc * lax.rsqrt(var + LN_EPS) * lng_ref[...] + lnb_ref[...]
    c = c * _sigmoid(c)
    yb = jnp.dot(c.astype(jnp.bfloat16), wco_ref[...], preferred_element_type=jnp.float32)

    merged = _sigmoid(z[:, s2:s3]) * ya2 + _sigmoid(z[:, s3:]) * yb
    h1 = xs + jnp.dot(merged.astype(jnp.bfloat16), wout_ref[...], preferred_element_type=jnp.float32)
    xs_ref[...] = h1
    for t in range(t_chunk):
        out_ref[:, pl.ds(t, 1), :] = xs_ref[t * bsz:(t + 1) * bsz, :].reshape(bsz, 1, d_model)


def _s5_discretize(a_re, a_im, log_dt, b_re, b_im, c_re, c_im):
    f32 = jnp.float32
    g, p = a_re.shape
    h = b_re.shape[-1]
    gps = LANES // h
    n_sb = g // gps
    dt = jnp.exp(log_dt.astype(f32))[:, None]
    decay = jnp.exp(a_re * dt)
    abar_re = decay * jnp.cos(a_im * dt)
    abar_im = decay * jnp.sin(a_im * dt)
    den = a_re * a_re + a_im * a_im
    nr = abar_re - 1.0
    coef_re = (nr * a_re + abar_im * a_im) / den
    coef_im = (abar_im * a_re - nr * a_im) / den
    bd_re = coef_re[..., None] * b_re - coef_im[..., None] * b_im
    bd_im = coef_re[..., None] * b_im + coef_im[..., None] * b_re
    eye = jnp.eye(gps, dtype=f32)

    def blockdiag_in(w):
        w = w.reshape(n_sb, gps, p, h)
        return jnp.einsum('sgph,gk->sghkp', w, eye).reshape(n_sb, gps * h, gps * p)

    def blockdiag_out(w):
        w = w.reshape(n_sb, gps, h, p)
        return jnp.einsum('sghp,gk->skpgh', w, eye).reshape(n_sb, gps * p, gps * h)

    bd = jnp.concatenate([blockdiag_in(bd_re), blockdiag_in(bd_im)], axis=-1)
    ct = jnp.concatenate([blockdiag_out(c_re), -blockdiag_out(c_im)], axis=1)
    return (abar_re.reshape(1, g * p), abar_im.reshape(1, g * p),
            bd.astype(jnp.bfloat16), ct.astype(jnp.bfloat16), gps * p)


def _mixer(x, meta_tokens, norm_g, w_in, a_re, a_im, log_dt, b_re, b_im, c_re, c_im, d_skip,
           w_s5_lin, w_s5_gate, dw_w, dw_b, ln_g, ln_b, w_conv_out, w_out, *, t_chunk):
    bsz, seq, d_model = x.shape
    n_meta = meta_tokens.shape[0]
    g, p = a_re.shape
    d_ssm = g * b_re.shape[-1]
    conv_w, _, d_conv = dw_w.shape
    assert bsz == SUBLANES and seq % t_chunk == 0 and t_chunk >= n_meta and t_chunk % SUBLANES == 0
    assert d_ssm % LANES == 0 and d_conv % LANES == 0
    f32, bf16 = jnp.float32, jnp.bfloat16
    ar, ai, bd, ct, sb_lanes = _s5_discretize(a_re.astype(f32), a_im.astype(f32), log_dt, b_re.astype(f32),
                                              b_im.astype(f32), c_re.astype(f32), c_im.astype(f32))
    meta_chunk = jnp.concatenate([jnp.zeros((t_chunk - n_meta, d_model), f32), meta_tokens.astype(f32)], axis=0)
    rows = t_chunk * bsz
    n_state = g * p
    consts = [
        meta_chunk, norm_g.reshape(1, d_model).astype(f32), w_in.astype(bf16), ar, ai, bd, ct,
        d_skip.reshape(1, d_ssm).astype(f32), w_s5_lin.astype(bf16), w_s5_gate.astype(bf16),
        dw_w.reshape(conv_w, d_conv).astype(f32), dw_b.reshape(1, d_conv).astype(f32),
        ln_g.reshape(1, d_conv).astype(f32), ln_b.reshape(1, d_conv).astype(f32),
        w_conv_out.astype(bf16), w_out.astype(bf16),
    ]
    x_spec = pl.BlockSpec((bsz, t_chunk, d_model), lambda i: (0, jnp.maximum(i - 1, 0), 0))
    kern = functools.partial(_mixer_kernel, t_chunk=t_chunk, d_ssm=d_ssm, d_conv=d_conv, conv_w=conv_w,
                             sb_lanes=sb_lanes)
    return pl.pallas_call(
        kern,
        out_shape=jax.ShapeDtypeStruct((bsz, seq, d_model), f32),
        grid=(seq // t_chunk + 1,),
        in_specs=[x_spec] + [_const_spec(c.shape) for c in consts],
        out_specs=x_spec,
        scratch_shapes=[
            pltpu.VMEM((rows, d_model), f32),
            pltpu.VMEM((rows, 2 * n_state), f32),
            pltpu.VMEM((bsz, 2 * n_state), f32),
            pltpu.VMEM(((conv_w - 1) * bsz + rows, d_conv), f32),
            pltpu.VMEM((rows, d_conv), f32),
        ],
        compiler_params=pltpu.CompilerParams(dimension_semantics=("arbitrary",), vmem_limit_bytes=VMEM_LIMIT),
        name="mixer",
    )(x.astype(f32), *consts)


def _pack_bf16_pairs(v):
    half = v.shape[1] // 2
    as_bits = lambda a: lax.bitcast_convert_type(a.astype(jnp.bfloat16).astype(jnp.float32), jnp.uint32)
    return (as_bits(v[:, half:]) & jnp.uint32(0xFFFF0000)) | (as_bits(v[:, :half]) >> 16)


def _unpack_bf16_pairs(w):
    lo = lax.bitcast_convert_type(w << 16, jnp.float32)
    hi = lax.bitcast_convert_type(w & jnp.uint32(0xFFFF0000), jnp.float32)
    return lo, hi


def _router_kernel(h_ref, g_ref, rwh_ref, rwl_ref, rb_ref,
                   xp_ref, idx_ref, prob_ref, rank_ref, cnt_ref,
                   carry_ref, *, n_experts):
    i = pl.program_id(0)
    tm = h_ref.shape[0]

    @pl.when(i == 0)
    def _():
        carry_ref[...] = jnp.zeros_like(carry_ref)

    h = h_ref[...]
    hn = h * lax.rsqrt(jnp.mean(h * h, axis=-1, keepdims=True) + RMS_EPS) * g_ref[...]
    xp_ref[...] = _pack_bf16_pairs(hn)
    hn_hi = hn.astype(jnp.bfloat16)
    hn_lo = (hn - hn_hi.astype(jnp.float32)).astype(jnp.bfloat16)
    nt = (((1,), (1,)), ((), ()))
    dot = lambda a, b: lax.dot_general(a, b, nt, preferred_element_type=jnp.float32)
    lg = dot(rwh_ref[...], hn_hi) + dot(rwh_ref[...], hn_lo) + dot(rwl_ref[...], hn_hi) + rb_ref[...]
    eidx = lax.broadcasted_iota(jnp.int32, (n_experts, tm), 0)
    work = lg
    vals, sels, hots = [], [], []
    for _ in range(TOP_K):
        m = jnp.max(work, axis=0, keepdims=True)
        sel = jnp.min(jnp.where(work == m, eidx, n_experts), axis=0, keepdims=True)
        hot = eidx == sel
        vals.append(m)
        sels.append(sel)
        hots.append(hot)
        work = jnp.where(hot, -jnp.inf, work)
    exps = [jnp.exp(v - vals[0]) for v in vals]
    tot = exps[0]
    for e in exps[1:]:
        tot = tot + e
    inv = 1.0 / tot
    oh = hots[0].astype(jnp.float32)
    for hot in hots[1:]:
        oh = oh + hot.astype(jnp.float32)
    upper = (lax.broadcasted_iota(jnp.int32, (tm, tm), 0) < lax.broadcasted_iota(jnp.int32, (tm, tm), 1))
    cum = jnp.dot(oh.astype(jnp.bfloat16), upper.astype(jnp.bfloat16), preferred_element_type=jnp.float32)
    cum = cum + carry_ref[:, 0:1]
    for k in range(TOP_K):
        idx_ref[k:k + 1, :] = sels[k]
        prob_ref[k:k + 1, :] = exps[k] * inv
        rank_ref[k:k + 1, :] = jnp.sum(jnp.where(hots[k], cum, 0.0), axis=0, keepdims=True).astype(jnp.int32)
    carry_ref[...] = carry_ref[...] + jnp.sum(oh, axis=1, keepdims=True)
    cnt_ref[...] = carry_ref[...]


def _router(h1, norm_g, router_w, router_b, *, tm):
    n, d = h1.shape
    e = router_w.shape[-1]
    f32 = jnp.float32
    rwt = router_w.astype(f32).T
    rwh = rwt.astype(jnp.bfloat16)
    rwl = (rwt - rwh.astype(f32)).astype(jnp.bfloat16)
    kn = lambda dt: jax.ShapeDtypeStruct((TOP_K, n), dt)
    kspec = pl.BlockSpec((TOP_K, tm), lambda i: (0, i))
    return pl.pallas_call(
        functools.partial(_router_kernel, n_experts=e),
        out_shape=(jax.ShapeDtypeStruct((n, d // 2), jnp.uint32), kn(jnp.int32), kn(f32), kn(jnp.int32),
                   jax.ShapeDtypeStruct((e, LANES), f32)),
        grid=(n // tm,),
        in_specs=[pl.BlockSpec((tm, d), lambda i: (i, 0)), _const_spec((1, d)), _const_spec((e, d)),
                  _const_spec((e, d)), _const_spec((e, 1))],
        out_specs=(pl.BlockSpec((tm, d // 2), lambda i: (i, 0)), kspec, kspec, kspec, _const_spec((e, LANES))),
        scratch_shapes=[pltpu.VMEM((e, LANES), f32)],
        compiler_params=pltpu.CompilerParams(dimension_semantics=("arbitrary",), vmem_limit_bytes=VMEM_LIMIT),
        name="router",
    )(h1, norm_g.reshape(1, d).astype(f32), rwh, rwl, router_b.reshape(e, 1).astype(f32))


def _dispatch_kernel(pos_ref, xp_ref, xs_ref, sem, *, tm):
    def start(j, c):
        for k in range(TOP_K):
            p = pos_ref[0, 0, k * tm + j]
            pltpu.make_async_copy(xp_ref.at[pl.ds(j, 1), :], xs_ref.at[pl.ds(p, 1), :], sem).start()
        return c

    lax.fori_loop(0, tm, start, 0)

    def wait(j, c):
        for k in range(TOP_K):
            pltpu.make_async_copy(xp_ref.at[pl.ds(0, 1), :], xs_ref.at[pl.ds(0, 1), :], sem).wait()
        return c

    lax.fori_loop(0, tm, wait, 0)


def _dispatch(xp, pos_tiles, m_pad, *, tm):
    n, half = xp.shape
    return pl.pallas_call(
        functools.partial(_dispatch_kernel, tm=tm),
        out_shape=jax.ShapeDtypeStruct((m_pad, half), jnp.uint32),
        grid=(n // tm,),
        in_specs=[pl.BlockSpec((1, 1, TOP_K * tm), lambda i: (i, 0, 0), memory_space=pltpu.SMEM),
                  pl.BlockSpec((tm, half), lambda i: (i, 0))],
        out_specs=pl.BlockSpec(memory_space=pl.ANY),
        scratch_shapes=[pltpu.SemaphoreType.DMA],
        compiler_params=pltpu.CompilerParams(dimension_semantics=("arbitrary",)),
        name="dispatch",
    )(pos_tiles, xp)


def _expert_kernel(te_ref, nu_ref, xs_ref, wgu_ref, bgu_ref, wd_ref, bd_ref, ys_ref):
    i = pl.program_id(0)

    @pl.when(i < nu_ref[0])
    def _():
        half = xs_ref.shape[1]
        d_exp = wd_ref.shape[0]
        lo, hi = _unpack_bf16_pairs(xs_ref[...])
        gu = (jnp.dot(lo.astype(jnp.bfloat16), wgu_ref[:half, :], preferred_element_type=jnp.float32)
              + jnp.dot(hi.astype(jnp.bfloat16), wgu_ref[half:, :], preferred_element_type=jnp.float32)
              + bgu_ref[...])
        g = jnp.minimum(gu[:, :d_exp], SWIGLU_LIMIT)
        lin = jnp.clip(gu[:, d_exp:], -SWIGLU_LIMIT, SWIGLU_LIMIT)
        act = g * _sigmoid(SWIGLU_ALPHA * g) * (lin + 1.0)
        y = jnp.dot(act.astype(jnp.bfloat16), wd_ref[...], preferred_element_type=jnp.float32) + bd_ref[...]
        ys_ref[...] = _pack_bf16_pairs(y)


def _experts(xs, tile_expert, n_used, w_gate_up, b_gate_up, w_down, b_down, *, tm):
    m_pad, half = xs.shape
    e, d, d_gu = w_gate_up.shape
    d_exp = w_down.shape[1]
    bf16, f32 = jnp.bfloat16, jnp.float32
    row_map = lambda i, te, nu: (jnp.minimum(i, nu[0] - 1), 0)
    w_map = lambda i, te, nu: (te[i], 0, 0)
    grid_spec = pltpu.PrefetchScalarGridSpec(
        num_scalar_prefetch=2,
        grid=(m_pad // tm,),
        in_specs=[pl.BlockSpec((tm, half), row_map),
                  pl.BlockSpec((None, d, d_gu), w_map), pl.BlockSpec((None, 1, d_gu), w_map),
                  pl.BlockSpec((None, d_exp, d), w_map), pl.BlockSpec((None, 1, d), w_map)],
        out_specs=pl.BlockSpec((tm, half), row_map),
    )
    return pl.pallas_call(
        _expert_kernel,
        out_shape=jax.ShapeDtypeStruct((m_pad, half), jnp.uint32),
        grid_spec=grid_spec,
        compiler_params=pltpu.CompilerParams(dimension_semantics=("arbitrary",), vmem_limit_bytes=VMEM_LIMIT),
        name="experts",
    )(tile_expert, n_used, xs, w_gate_up.astype(bf16), b_gate_up.reshape(e, 1, d_gu).astype(f32),
      w_down.astype(bf16), b_down.reshape(e, 1, d).astype(f32))


def _combine_kernel(pos_ref, h_ref, p_ref, g_ref, ys_ref, out_ref, buf_ref, sem, *, tm):
    def start(j, c):
        for k in range(TOP_K):
            p = pos_ref[0, 0, k * tm + j]
            pltpu.make_async_copy(ys_ref.at[pl.ds(p, 1), :], buf_ref.at[k, pl.ds(j, 1), :], sem).start()
        return c

    lax.fori_loop(0, tm, start, 0)

    def wait(j, c):
        for k in range(TOP_K):
            pltpu.make_async_copy(ys_ref.at[pl.ds(0, 1), :], buf_ref.at[0, pl.ds(0, 1), :], sem).wait()
        return c

    lax.fori_loop(0, tm, wait, 0)
    half = buf_ref.shape[2]
    h = h_ref[...]
    acc_lo = h[:, :half]
    acc_hi = h[:, half:]
    for k in range(TOP_K):
        lo, hi = _unpack_bf16_pairs(buf_ref[k])
        pk = p_ref[:, k:k + 1]
        acc_lo = acc_lo + pk * lo
        acc_hi = acc_hi + pk * hi
    ms = (jnp.sum(acc_lo * acc_lo, axis=-1, keepdims=True)
          + jnp.sum(acc_hi * acc_hi, axis=-1, keepdims=True)) * (1.0 / (2 * half))
    scale = lax.rsqrt(ms + RMS_EPS)
    out_ref[:, :half] = acc_lo * scale * g_ref[:, :half]
    out_ref[:, half:] = acc_hi * scale * g_ref[:, half:]


def _combine(h1, prob_t, pos_tiles, ys, norm_g, *, tm):
    n, d = h1.shape
    half = d // 2
    f32 = jnp.float32
    return pl.pallas_call(
        functools.partial(_combine_kernel, tm=tm),
        out_shape=jax.ShapeDtypeStruct((n, d), f32),
        grid=(n // tm,),
        in_specs=[pl.BlockSpec((1, 1, TOP_K * tm), lambda i: (i, 0, 0), memory_space=pltpu.SMEM),
                  pl.BlockSpec((tm, d), lambda i: (i, 0)),
                  pl.BlockSpec((tm, TOP_K), lambda i: (i, 0)),
                  _const_spec((1, d)),
                  pl.BlockSpec(memory_space=pl.ANY)],
        out_specs=pl.BlockSpec((tm, d), lambda i: (i, 0)),
        scratch_shapes=[pltpu.VMEM((TOP_K, tm, half), jnp.uint32), pltpu.SemaphoreType.DMA],
        compiler_params=pltpu.CompilerParams(dimension_semantics=("arbitrary",)),
        name="combine",
    )(pos_tiles, h1, prob_t, norm_g.reshape(1, d).astype(f32), ys)


def _tile_rows(n, target):
    t = min(target, n)
    while n % t:
        t //= 2
    return t


def _moe_and_final_norm(h1, norm_ffn_g, router_w, router_b, w_gate_up, b_gate_up, w_down, b_down, norm_final_g):
    n, d = h1.shape
    e = router_w.shape[-1]
    tm_r = _tile_rows(n, 512)
    tm_d = _tile_rows(n, 256)
    tm_e = _tile_rows(n, 512)
    xp, idx, prob, rank, cnt = _router(h1, norm_ffn_g, router_w, router_b, tm=tm_r)
    counts = cnt[:, 0].astype(jnp.int32)
    tiles_per_e = (counts + tm_e - 1) // tm_e
    tile_end = jnp.cumsum(tiles_per_e)
    start = (tile_end - tiles_per_e) * tm_e
    n_used = tile_end[-1:]
    n_tiles = (TOP_K * n) // tm_e + e
    tile_id = jnp.minimum(jnp.arange(n_tiles, dtype=jnp.int32), n_used[0] - 1)
    tile_expert = jnp.minimum(jnp.searchsorted(tile_end, tile_id, side='right'), e - 1).astype(jnp.int32)
    pos = jnp.take(start, idx) + rank
    pos_tiles = pos.reshape(TOP_K, n // tm_d, tm_d).transpose(1, 0, 2).reshape(n // tm_d, 1, TOP_K * tm_d)
    xs = _dispatch(xp, pos_tiles, n_tiles * tm_e, tm=tm_d)
    ys = _experts(xs, tile_expert, n_used.astype(jnp.int32), w_gate_up, b_gate_up, w_down, b_down, tm=tm_e)
    return _combine(h1, prob.T, pos_tiles, ys, norm_final_g, tm=tm_d)


def kernel(x, meta_tokens, norm_mix_g, w_in, ssm_a_re, ssm_a_im, ssm_log_dt, ssm_b_re, ssm_b_im, ssm_c_re, ssm_c_im, ssm_d, w_s5_lin, w_s5_gate, conv_dw_w, conv_dw_b, conv_ln_g, conv_ln_b, w_conv_out, w_out, norm_ffn_g, router_w, router_b, w_gate_up, b_gate_up, w_down, b_down, norm_final_g):
    assert norm_mix_g.shape[0] == 1, "single-layer block"
    bsz, seq, d_model = x.shape
    h1 = _mixer(x, meta_tokens, norm_mix_g[0], w_in[0], ssm_a_re[0], ssm_a_im[0], ssm_log_dt[0], ssm_b_re[0],
                ssm_b_im[0], ssm_c_re[0], ssm_c_im[0], ssm_d[0], w_s5_lin[0], w_s5_gate[0], conv_dw_w[0],
                conv_dw_b[0], conv_ln_g[0], conv_ln_b[0], w_conv_out[0], w_out[0], t_chunk=32)
    out = _moe_and_final_norm(h1.reshape(bsz * seq, d_model), norm_ffn_g[0], router_w[0], router_b[0],
                              w_gate_up[0], b_gate_up[0], w_down[0], b_down[0], norm_final_g)
    return out.reshape(bsz, seq, d_model).astype(x.dtype)
```

```python
import functools
import math

import jax
import jax.numpy as jnp
from jax import lax
from jax.experimental import pallas as pl
from jax.experimental.pallas import tpu as pltpu

RMS_EPS = 1e-6
LN_EPS = 1e-5
TOP_K = 4
SWIGLU_LIMIT = 7.0
SWIGLU_ALPHA = 1.702

LANES = 128
SUBLANES = 8
VMEM_LIMIT = 56 * 1024 * 1024


def _sigmoid(v):
    return 1.0 / (1.0 + jnp.exp(-v))


def _const_spec(shape):
    nd = len(shape)
    return pl.BlockSpec(shape, lambda *_: (0,) * nd)


def _mixer_kernel(x_ref, meta_ref, g_ref, win_ref, ar_ref, ai_ref, bd_ref, ct_ref, dskip_ref,
                  wlin_ref, wgate_ref, dww_ref, dwb_ref, lng_ref, lnb_ref, wco_ref, wout_ref,
                  out_ref,
                  xs_ref, st_ref, state_ref, cbuf_ref, conv_ref,
                  *, t_chunk, d_ssm, d_conv, conv_w, sb_lanes):
    i = pl.program_id(0)
    bsz = SUBLANES
    rows = t_chunk * bsz
    d_model = xs_ref.shape[1]
    n_sb = d_ssm // LANES
    hist = (conv_w - 1) * bsz

    @pl.when(i == 0)
    def _():
        state_ref[...] = jnp.zeros_like(state_ref)
        cbuf_ref[...] = jnp.zeros_like(cbuf_ref)
        for t in range(t_chunk):
            xs_ref[t * bsz:(t + 1) * bsz, :] = jnp.broadcast_to(meta_ref[t:t + 1, :], (bsz, d_model))

    @pl.when(i > 0)
    def _():
        for t in range(t_chunk):
            xs_ref[t * bsz:(t + 1) * bsz, :] = x_ref[:, pl.ds(t, 1), :].reshape(bsz, d_model)

    xs = xs_ref[...]
    hn = xs * lax.rsqrt(jnp.mean(xs * xs, axis=-1, keepdims=True) + RMS_EPS) * g_ref[...]
    z = jnp.dot(hn.astype(jnp.bfloat16), win_ref[...], preferred_element_type=jnp.float32)
    s1 = d_ssm
    s2 = s1 + 2 * d_conv
    s3 = s2 + d_model

    u = z[:, :s1]
    u_bf = u.astype(jnp.bfloat16)
    for sb in range(n_sb):
        st_ref[:, sb * 2 * sb_lanes:(sb + 1) * 2 * sb_lanes] = jnp.dot(
            u_bf[:, sb * LANES:(sb + 1) * LANES], bd_ref[sb], preferred_element_type=jnp.float32)
    n_lb = sb_lanes // LANES
    for sb in range(n_sb):
        base = sb * 2 * sb_lanes
        for j in range(n_lb):
            lre = slice(base + j * LANES, base + (j + 1) * LANES)
            lim = slice(base + sb_lanes + j * LANES, base + sb_lanes + (j + 1) * LANES)
            la = slice(sb * sb_lanes + j * LANES, sb * sb_lanes + (j + 1) * LANES)
            ar = jnp.broadcast_to(ar_ref[:, la], (bsz, LANES))
            ai = jnp.broadcast_to(ai_ref[:, la], (bsz, LANES))
            sre = state_ref[:, lre]
            sim = state_ref[:, lim]
            for t in range(t_chunk):
                r = slice(t * bsz, (t + 1) * bsz)
                nre = ar * sre - ai * sim + st_ref[r, lre]
                nim = ar * sim + ai * sre + st_ref[r, lim]
                st_ref[r, lre] = nre
                st_ref[r, lim] = nim
                sre, sim = nre, nim
            state_ref[:, lre] = sre
            state_ref[:, lim] = sim
    ys = []
    for sb in range(n_sb):
        ys.append(jnp.dot(st_ref[:, sb * 2 * sb_lanes:(sb + 1) * 2 * sb_lanes].astype(jnp.bfloat16),
                          ct_ref[sb], preferred_element_type=jnp.float32))
    y = jnp.concatenate(ys, axis=-1) if n_sb > 1 else ys[0]
    y = y + dskip_ref[...] * u
    ya = 0.5 * y * (1.0 + lax.erf(y * (1.0 / math.sqrt(2.0))))
    ya_bf = ya.astype(jnp.bfloat16)
    ya2 = (jnp.dot(ya_bf, wlin_ref[...], preferred_element_type=jnp.float32)
           * _sigmoid(jnp.dot(ya_bf, wgate_ref[...], preferred_element_type=jnp.float32)))

    cbuf_ref[hist:hist + rows, :] = z[:, s1:s1 + d_conv] * _sigmoid(z[:, s1 + d_conv:s2])
    for lb in range(d_conv // LANES):
        ls = slice(lb * LANES, (lb + 1) * LANES)
        wk = [jnp.broadcast_to(dww_ref[k:k + 1, ls], (bsz, LANES)) for k in range(conv_w)]
        for t in range(t_chunk):
            acc = wk[0] * cbuf_ref[t * bsz:(t + 1) * bsz, ls]
            for k in range(1, conv_w):
                acc = acc + wk[k] * cbuf_ref[(t + k) * bsz:(t + k + 1) * bsz, ls]
            conv_ref[t * bsz:(t + 1) * bsz, ls] = acc
    cbuf_ref[0:hist, :] = cbuf_ref[rows:rows + hist, :]
    c = conv_ref[...] + dwb_ref[...]
    mu = jnp.mean(c, axis=-1, keepdims=True)
    cc = c - mu
    var = jnp.mean(cc * cc, axis=-1, keepdims=True)
    c = cc * lax.rsqrt(var + LN_EPS) * lng_ref[...] + lnb_ref[...]
    c = c * _sigmoid(c)
    yb = jnp.dot(c.astype(jnp.bfloat16), wco_ref[...], preferred_element_type=jnp.float32)

    merged = _sigmoid(z[:, s2:s3]) * ya2 + _sigmoid(z[:, s3:]) * yb
    h1 = xs + jnp.dot(merged.astype(jnp.bfloat16), wout_ref[...], preferred_element_type=jnp.float32)
    xs_ref[...] = h1
    for t in range(t_chunk):
        out_ref[:, pl.ds(t, 1), :] = xs_ref[t * bsz:(t + 1) * bsz, :].reshape(bsz, 1, d_model)


def _s5_discretize(a_re, a_im, log_dt, b_re, b_im, c_re, c_im):
    f32 = jnp.float32
    g, p = a_re.shape
    h = b_re.shape[-1]
    gps = LANES // h
    n_sb = g // gps
    dt = jnp.exp(log_dt.astype(f32))[:, None]
    decay = jnp.exp(a_re * dt)
    abar_re = decay * jnp.cos(a_im * dt)
    abar_im = decay * jnp.sin(a_im * dt)
    den = a_re * a_re + a_im * a_im
    nr = abar_re - 1.0
    coef_re = (nr * a_re + abar_im * a_im) / den
    coef_im = (abar_im * a_re - nr * a_im) / den
    bd_re = coef_re[..., None] * b_re - coef_im[..., None] * b_im
    bd_im = coef_re[..., None] * b_im + coef_im[..., None] * b_re
    eye = jnp.eye(gps, dtype=f32)

    def blockdiag_in(w):
        w = w.reshape(n_sb, gps, p, h)
        return jnp.einsum('sgph,gk->sghkp', w, eye).reshape(n_sb, gps * h, gps * p)

    def blockdiag_out(w):
        w = w.reshape(n_sb, gps, h, p)
        return jnp.einsum('sghp,gk->skpgh', w, eye).reshape(n_sb, gps * p, gps * h)

    bd = jnp.concatenate([blockdiag_in(bd_re), blockdiag_in(bd_im)], axis=-1)
    ct = jnp.concatenate([blockdiag_out(c_re), -blockdiag_out(c_im)], axis=1)
    return (abar_re.reshape(1, g * p), abar_im.reshape(1, g * p),
            bd.astype(jnp.bfloat16), ct.astype(jnp.bfloat16), gps * p)


def _mixer(x, meta_tokens, norm_g, w_in, a_re, a_im, log_dt, b_re, b_im, c_re, c_im, d_skip,
           w_s5_lin, w_s5_gate, dw_w, dw_b, ln_g, ln_b, w_conv_out, w_out, *, t_chunk):
    bsz, seq, d_model = x.shape
    n_meta = meta_tokens.shape[0]
    g, p = a_re.shape
    d_ssm = g * b_re.shape[-1]
    conv_w, _, d_conv = dw_w.shape
    assert bsz == SUBLANES and seq % t_chunk == 0 and t_chunk >= n_meta and t_chunk % SUBLANES == 0
    assert d_ssm % LANES == 0 and d_conv % LANES == 0
    f32, bf16 = jnp.float32, jnp.bfloat16
    ar, ai, bd, ct, sb_lanes = _s5_discretize(a_re.astype(f32), a_im.astype(f32), log_dt, b_re.astype(f32),
                                              b_im.astype(f32), c_re.astype(f32), c_im.astype(f32))
    meta_chunk = jnp.concatenate([jnp.zeros((t_chunk - n_meta, d_model), f32), meta_tokens.astype(f32)], axis=0)
    rows = t_chunk * bsz
    n_state = g * p
    consts = [
        meta_chunk, norm_g.reshape(1, d_model).astype(f32), w_in.astype(bf16), ar, ai, bd, ct,
        d_skip.reshape(1, d_ssm).astype(f32), w_s5_lin.astype(bf16), w_s5_gate.astype(bf16),
        dw_w.reshape(conv_w, d_conv).astype(f32), dw_b.reshape(1, d_conv).astype(f32),
        ln_g.reshape(1, d_conv).astype(f32), ln_b.reshape(1, d_conv).astype(f32),
        w_conv_out.astype(bf16), w_out.astype(bf16),
    ]
    x_spec = pl.BlockSpec((bsz, t_chunk, d_model), lambda i: (0, jnp.maximum(i - 1, 0), 0))
    kern = functools.partial(_mixer_kernel, t_chunk=t_chunk, d_ssm=d_ssm, d_conv=d_conv, conv_w=conv_w,
                             sb_lanes=sb_lanes)
    return pl.pallas_call(
        kern,
        out_shape=jax.ShapeDtypeStruct((bsz, seq, d_model), f32),
        grid=(seq // t_chunk + 1,),
        in_specs=[x_spec] + [_const_spec(c.shape) for c in consts],
        out_specs=x_spec,
        scratch_shapes=[
            pltpu.VMEM((rows, d_model), f32),
            pltpu.VMEM((rows, 2 * n_state), f32),
            pltpu.VMEM((bsz, 2 * n_state), f32),
            pltpu.VMEM(((conv_w - 1) * bsz + rows, d_conv), f32),
            pltpu.VMEM((rows, d_conv), f32),
        ],
        compiler_params=pltpu.CompilerParams(dimension_semantics=("arbitrary",), vmem_limit_bytes=VMEM_LIMIT),
        name="mixer",
    )(x.astype(f32), *consts)


RUN_CHUNK = 32


def _router_kernel(h_ref, g_ref, rwh_ref, rwl_ref, rb_ref, xb_ref, lp_ref, prob_ref, cnt_ref, *, n_experts):
    tm = h_ref.shape[0]
    h = h_ref[...]
    hn = h * lax.rsqrt(jnp.mean(h * h, axis=-1, keepdims=True) + RMS_EPS) * g_ref[...]
    hn_hi = hn.astype(jnp.bfloat16)
    xb_ref[...] = hn_hi
    hn_lo = (hn - hn_hi.astype(jnp.float32)).astype(jnp.bfloat16)
    nt = (((1,), (1,)), ((), ()))
    dot = lambda a, b: lax.dot_general(a, b, nt, preferred_element_type=jnp.float32)
    lg = dot(rwh_ref[...], hn_hi) + dot(rwh_ref[...], hn_lo) + dot(rwl_ref[...], hn_hi) + rb_ref[...]
    eidx = lax.broadcasted_iota(jnp.int32, (n_experts, tm), 0)
    work = lg
    vals, hots = [], []
    for _ in range(TOP_K):
        m = jnp.max(work, axis=0, keepdims=True)
        sel = jnp.min(jnp.where(work == m, eidx, n_experts), axis=0, keepdims=True)
        hot = eidx == sel
        vals.append(m)
        hots.append(hot)
        work = jnp.where(hot, -jnp.inf, work)
    exps = [jnp.exp(v - vals[0]) for v in vals]
    tot = exps[0]
    for e in exps[1:]:
        tot = tot + e
    inv = 1.0 / tot
    oh = hots[0].astype(jnp.float32)
    for hot in hots[1:]:
        oh = oh + hot.astype(jnp.float32)
    upper = (lax.broadcasted_iota(jnp.int32, (tm, tm), 0) < lax.broadcasted_iota(jnp.int32, (tm, tm), 1))
    cum = jnp.dot(oh.astype(jnp.bfloat16), upper.astype(jnp.bfloat16), preferred_element_type=jnp.float32)
    cnt = jnp.sum(oh, axis=1, keepdims=True)
    lower = (lax.broadcasted_iota(jnp.int32, (n_experts, n_experts), 1)
             < lax.broadcasted_iota(jnp.int32, (n_experts, n_experts), 0))
    cnt_b = jnp.broadcast_to(cnt, (n_experts, LANES))
    c_hi = jnp.floor(cnt_b * (1.0 / 16.0))
    c_lo = cnt_b - 16.0 * c_hi
    ldot = lambda v: jnp.dot(lower.astype(jnp.bfloat16), v.astype(jnp.bfloat16), preferred_element_type=jnp.float32)
    loff = 16.0 * ldot(c_hi) + ldot(c_lo)
    base = cum + loff[:, 0:1]
    for k in range(TOP_K):
        prob_ref[k:k + 1, :] = exps[k] * inv
        lp_ref[k:k + 1, :] = jnp.sum(jnp.where(hots[k], base, 0.0), axis=0, keepdims=True).astype(jnp.int32)
    cnt_ref[...] = cnt_b.astype(jnp.int32)


def _router(h1, norm_g, router_w, router_b, *, tm):
    n, d = h1.shape
    e = router_w.shape[-1]
    f32 = jnp.float32
    rwt = router_w.astype(f32).T
    rwh = rwt.astype(jnp.bfloat16)
    rwl = (rwt - rwh.astype(f32)).astype(jnp.bfloat16)
    kn = lambda dt: jax.ShapeDtypeStruct((TOP_K, n), dt)
    kspec = pl.BlockSpec((TOP_K, tm), lambda i: (0, i))
    return pl.pallas_call(
        functools.partial(_router_kernel, n_experts=e),
        out_shape=(jax.ShapeDtypeStruct((n, d), jnp.bfloat16), kn(jnp.int32), kn(f32),
                   jax.ShapeDtypeStruct((n // tm, e, LANES), jnp.int32)),
        grid=(n // tm,),
        in_specs=[pl.BlockSpec((tm, d), lambda i: (i, 0)), _const_spec((1, d)), _const_spec((e, d)),
                  _const_spec((e, d)), _const_spec((e, 1))],
        out_specs=(pl.BlockSpec((tm, d), lambda i: (i, 0)), kspec, kspec,
                   pl.BlockSpec((None, e, LANES), lambda i: (i, 0, 0))),
        compiler_params=pltpu.CompilerParams(dimension_semantics=("arbitrary",), vmem_limit_bytes=VMEM_LIMIT),
        name="router",
    )(h1, norm_g.reshape(1, d).astype(f32), rwh, rwl, router_b.reshape(e, 1).astype(f32))


def _sort_matrix(lp, n_rows, axis):
    if axis == 0:
        shape = (n_rows, lp.shape[1])
        hit = lambda k: lax.broadcasted_iota(jnp.int32, shape, 0) == lp[k:k + 1, :]
    else:
        shape = (lp.shape[0], n_rows)
        hit = lambda k: lax.broadcasted_iota(jnp.int32, shape, 1) == lp[:, k:k + 1]
    m = hit(0)
    for k in range(1, TOP_K):
        m = m | hit(k)
    return m


def _start_runs(n_runs, run_args, src_ref, dst_ref, sem, src_step=RUN_CHUNK):
    def one_run(r, started):
        s, d, ln = run_args(r)
        n_ch = (jnp.maximum(ln, 0) + (RUN_CHUNK - 1)) // RUN_CHUNK

        def one_chunk(q, c):
            pltpu.make_async_copy(src_ref.at[pl.ds(s + q * src_step, RUN_CHUNK)],
                                  dst_ref.at[pl.ds(d + q * RUN_CHUNK, RUN_CHUNK)], sem).start()
            return c

        lax.fori_loop(0, n_ch, one_chunk, 0)
        return started + n_ch

    return lax.fori_loop(0, n_runs, one_run, jnp.int32(0))


def _drain(count, src_ref, dst_ref, sem):
    def wait_one(q, c):
        pltpu.make_async_copy(src_ref.at[pl.ds(0, RUN_CHUNK)], dst_ref.at[pl.ds(0, RUN_CHUNK)], sem).wait()
        return c

    lax.fori_loop(0, count, wait_one, 0)


def _dispatch_kernel(cnt_ref, loff_ref, dst_ref, fill_at_ref, fill_len_ref, xb_ref, lp_ref, xs_ref,
                     xl_ref, pend_ref, sem, *, n_experts):
    i = pl.program_id(0)
    rows = xl_ref.shape[1] - RUN_CHUNK
    d = xl_ref.shape[3]

    @pl.when(i == 0)
    def _():
        for s in range(2):
            xl_ref[s, rows:] = jnp.zeros((RUN_CHUNK, 1, d), xl_ref.dtype)
        n_fill = _start_runs(n_experts + 1, lambda r: (rows, fill_at_ref[r], fill_len_ref[r]), xl_ref.at[0], xs_ref,
                             sem, src_step=0)
        _drain(n_fill, xl_ref.at[0], xs_ref, sem)
        pend_ref[0] = 0

    src = xl_ref.at[i % 2]
    perm = _sort_matrix(lp_ref[...], rows, 0).astype(jnp.bfloat16)
    src[:rows] = jnp.dot(perm, xb_ref[...], preferred_element_type=jnp.float32).reshape(rows, 1, d)

    def run_args(e):
        r = i * n_experts + e
        return loff_ref[r], dst_ref[r], cnt_ref[r]

    _drain(pend_ref[0], src, xs_ref, sem)
    pend_ref[0] = _start_runs(n_experts, run_args, src, xs_ref, sem)

    @pl.when(i == pl.num_programs(0) - 1)
    def _():
        _drain(pend_ref[0], src, xs_ref, sem)


def _dispatch(xb, lp, cnt_flat, loff_flat, dst_flat, fill_at, fill_len, m_pad, *, tm, n_experts):
    n, d = xb.shape
    rows = TOP_K * tm
    grid_spec = pltpu.PrefetchScalarGridSpec(
        num_scalar_prefetch=5,
        grid=(n // tm,),
        in_specs=[pl.BlockSpec((tm, d), lambda i, *_: (i, 0)), pl.BlockSpec((TOP_K, tm), lambda i, *_: (0, i))],
        out_specs=pl.BlockSpec(memory_space=pl.ANY),
        scratch_shapes=[pltpu.VMEM((2, rows + RUN_CHUNK, 1, d), jnp.float32), pltpu.SMEM((1,), jnp.int32),
                        pltpu.SemaphoreType.DMA],
    )
    return pl.pallas_call(
        functools.partial(_dispatch_kernel, n_experts=n_experts),
        out_shape=jax.ShapeDtypeStruct((m_pad, 1, d), jnp.float32),
        grid_spec=grid_spec,
        compiler_params=pltpu.CompilerParams(dimension_semantics=("arbitrary",), vmem_limit_bytes=VMEM_LIMIT),
        name="dispatch",
    )(cnt_flat, loff_flat, dst_flat, fill_at, fill_len, xb, lp)


def _expert_kernel(te_ref, nu_ref, tj_ref, ilo_ref, ihi_ref, cnt_ref, loff_ref, pre_ref,
                   xs_ref, wgu_ref, bgu_ref, wd_ref, bd_ref, ys_ref, x2_ref, yb_ref, pend_ref, sem,
                   *, n_experts, blk_rows):
    j = pl.program_id(0)
    tm, _, d = xs_ref.shape

    @pl.when(j == 0)
    def _():
        for s in range(2):
            yb_ref[s, tm:] = jnp.zeros((RUN_CHUNK, 1, d), yb_ref.dtype)
        n_blocks = ys_ref.shape[0] // blk_rows
        n_fill = _start_runs(n_blocks, lambda r: (tm, (r + 1) * blk_rows - RUN_CHUNK, RUN_CHUNK), yb_ref.at[0],
                             ys_ref, sem, src_step=0)
        _drain(n_fill, yb_ref.at[0], ys_ref, sem)
        pend_ref[0] = 0

    src = yb_ref.at[j % 2]

    @pl.when(j < nu_ref[0])
    def _():
        d_exp = wd_ref.shape[0]
        x2_ref[...] = xs_ref[...].reshape(tm, d)
        gu = jnp.dot(x2_ref[...].astype(jnp.bfloat16), wgu_ref[...], preferred_element_type=jnp.float32) + bgu_ref[...]
        g = jnp.minimum(gu[:, :d_exp], SWIGLU_LIMIT)
        lin = jnp.clip(gu[:, d_exp:], -SWIGLU_LIMIT, SWIGLU_LIMIT)
        act = g * _sigmoid(SWIGLU_ALPHA * g) * (lin + 1.0)
        y = jnp.dot(act.astype(jnp.bfloat16), wd_ref[...], preferred_element_type=jnp.float32) + bd_ref[...]
        src[:tm] = y.reshape(tm, 1, d)

    _drain(pend_ref[0], src, ys_ref, sem)
    pend_ref[0] = 0

    @pl.when(j < nu_ref[0])
    def _():
        e = te_ref[j]
        base = tj_ref[j] * tm
        i0 = ilo_ref[j]

        def run_args(r):
            it = i0 + r
            a = pre_ref[it * n_experts + e] - base
            b = a + cnt_ref[it * n_experts + e]
            ac = jnp.maximum(a, 0)
            return ac, it * blk_rows + loff_ref[it * n_experts + e] + (ac - a), jnp.minimum(b, tm) - ac

        pend_ref[0] = _start_runs(ihi_ref[j] - i0, run_args, src, ys_ref, sem)

    @pl.when(j == pl.num_programs(0) - 1)
    def _():
        _drain(pend_ref[0], src, ys_ref, sem)


def _experts(xs, sched, w_gate_up, b_gate_up, w_down, b_down, n_tok_tiles, blk_rows, *, tm):
    m_pad = xs.shape[0]
    e, d, d_gu = w_gate_up.shape
    d_exp = w_down.shape[1]
    bf16, f32 = jnp.bfloat16, jnp.float32
    row_map = lambda j, te, nu, *_: (jnp.minimum(j, nu[0] - 1), 0, 0)
    w_map = lambda j, te, *_: (te[j], 0, 0)
    grid_spec = pltpu.PrefetchScalarGridSpec(
        num_scalar_prefetch=len(sched),
        grid=(m_pad // tm,),
        in_specs=[pl.BlockSpec((tm, 1, d), row_map),
                  pl.BlockSpec((None, d, d_gu), w_map), pl.BlockSpec((None, 1, d_gu), w_map),
                  pl.BlockSpec((None, d_exp, d), w_map), pl.BlockSpec((None, 1, d), w_map)],
        out_specs=pl.BlockSpec(memory_space=pl.ANY),
        scratch_shapes=[pltpu.VMEM((tm, d), f32), pltpu.VMEM((2, tm + RUN_CHUNK, 1, d), f32),
                        pltpu.SMEM((1,), jnp.int32), pltpu.SemaphoreType.DMA],
    )
    return pl.pallas_call(
        functools.partial(_expert_kernel, n_experts=e, blk_rows=blk_rows),
        out_shape=jax.ShapeDtypeStruct((n_tok_tiles * blk_rows, 1, d), f32),
        grid_spec=grid_spec,
        compiler_params=pltpu.CompilerParams(dimension_semantics=("arbitrary",), vmem_limit_bytes=VMEM_LIMIT),
        name="experts",
    )(*sched, xs, w_gate_up.astype(bf16), b_gate_up.reshape(e, 1, d_gu).astype(f32),
      w_down.astype(bf16), b_down.reshape(e, 1, d).astype(f32))


def _combine_kernel(h_ref, ys_ref, lp_ref, lpt_ref, p_ref, g_ref, out_ref, y2_ref):
    tm, d = h_ref.shape
    rows = TOP_K * tm
    y2_ref[...] = ys_ref[:rows].reshape(rows, d)
    lp = lp_ref[...]
    riota = lax.broadcasted_iota(jnp.int32, (rows, tm), 0)
    wmat = jnp.where(riota == lp[0:1, :], p_ref[0:1, :], 0.0)
    for k in range(1, TOP_K):
        wmat = wmat + jnp.where(riota == lp[k:k + 1, :], p_ref[k:k + 1, :], 0.0)
    pw = jnp.sum(wmat, axis=1, keepdims=True)
    unsort = _sort_matrix(lpt_ref[...], rows, 1).astype(jnp.bfloat16)
    acc = h_ref[...] + jnp.dot(unsort, (y2_ref[...] * pw).astype(jnp.bfloat16), preferred_element_type=jnp.float32)
    out_ref[...] = acc * lax.rsqrt(jnp.mean(acc * acc, axis=-1, keepdims=True) + RMS_EPS) * g_ref[...]


def _combine(h1, ys, lp, lpt, prob, norm_g, blk_rows, *, tm):
    n, d = h1.shape
    f32 = jnp.float32
    kspec = pl.BlockSpec((TOP_K, tm), lambda i: (0, i))
    return pl.pallas_call(
        _combine_kernel,
        out_shape=jax.ShapeDtypeStruct((n, d), f32),
        grid=(n // tm,),
        in_specs=[pl.BlockSpec((tm, d), lambda i: (i, 0)),
                  pl.BlockSpec((blk_rows, 1, d), lambda i: (i, 0, 0)),
                  kspec, pl.BlockSpec((tm, TOP_K), lambda i: (i, 0)), kspec, _const_spec((1, d))],
        out_specs=pl.BlockSpec((tm, d), lambda i: (i, 0)),
        scratch_shapes=[pltpu.VMEM((TOP_K * tm, d), f32)],
        compiler_params=pltpu.CompilerParams(dimension_semantics=("arbitrary",), vmem_limit_bytes=VMEM_LIMIT),
        name="combine",
    )(h1, ys, lp, lpt, prob, norm_g.reshape(1, d).astype(f32))


def _tile_rows(n, target):
    t = min(target, n)
    while n % t:
        t //= 2
    return t


def _moe_schedule(cnt, tm_tok, tm_e):
    n_t, e = cnt.shape
    i32 = jnp.int32
    total = jnp.sum(cnt, axis=0)
    tiles_per_e = (total + RUN_CHUNK + tm_e - 1) // tm_e
    tile_end = jnp.cumsum(tiles_per_e)
    tile_begin = tile_end - tiles_per_e
    n_used = tile_end[-1:]
    n_tiles = (TOP_K * n_t * tm_tok) // tm_e + 2 * e
    tile_id = jnp.minimum(jnp.arange(n_tiles, dtype=i32), n_used[0] - 1)
    tile_expert = jnp.sum(tile_id[:, None] >= tile_end[None, :], axis=1).astype(i32)
    tile_j = tile_id - tile_begin[tile_expert]
    pre = jnp.cumsum(cnt, axis=0) - cnt
    loff = jnp.cumsum(cnt, axis=1) - cnt
    dst = tile_begin[None, :] * tm_e + pre
    pre_t = pre[:, tile_expert].T
    end_t = pre_t + cnt[:, tile_expert].T
    ilo = jnp.sum(end_t <= (tile_j * tm_e)[:, None], axis=1).astype(i32)
    ihi = jnp.sum(pre_t < ((tile_j + 1) * tm_e)[:, None], axis=1).astype(i32)
    gap = tiles_per_e * tm_e - total
    fill_len = (gap + RUN_CHUNK - 1) // RUN_CHUNK * RUN_CHUNK
    fill_at = jnp.concatenate([tile_end * tm_e - fill_len, n_used * tm_e])
    fill_len = jnp.concatenate([fill_len, (n_tiles - n_used) * tm_e])
    flat = lambda a: a.reshape(-1).astype(i32)
    return dict(n_tiles=n_tiles, tile_expert=tile_expert, n_used=n_used.astype(i32), tile_j=tile_j.astype(i32),
                ilo=ilo, ihi=ihi, cnt=flat(cnt), loff=flat(loff), pre=flat(pre), dst=flat(dst),
                fill_at=flat(fill_at), fill_len=flat(fill_len))


def _moe_and_final_norm(h1, norm_ffn_g, router_w, router_b, w_gate_up, b_gate_up, w_down, b_down, norm_final_g):
    n, d = h1.shape
    e = router_w.shape[-1]
    tm_tok = _tile_rows(n, 256)
    tm_e = _tile_rows(n, 512)
    blk_rows = TOP_K * tm_tok + RUN_CHUNK
    xb, lp, prob, cnt3 = _router(h1, norm_ffn_g, router_w, router_b, tm=tm_tok)
    s = _moe_schedule(cnt3[:, :, 0], tm_tok, tm_e)
    xs = _dispatch(xb, lp, s["cnt"], s["loff"], s["dst"], s["fill_at"], s["fill_len"], s["n_tiles"] * tm_e,
                   tm=tm_tok, n_experts=e)
    sched = (s["tile_expert"], s["n_used"], s["tile_j"], s["ilo"], s["ihi"], s["cnt"], s["loff"], s["pre"])
    ys = _experts(xs, sched, w_gate_up, b_gate_up, w_down, b_down, n // tm_tok, blk_rows, tm=tm_e)
    return _combine(h1, ys, lp, lp.T, prob, norm_final_g, blk_rows, tm=tm_tok)


def kernel(x, meta_tokens, norm_mix_g, w_in, ssm_a_re, ssm_a_im, ssm_log_dt, ssm_b_re, ssm_b_im, ssm_c_re, ssm_c_im, ssm_d, w_s5_lin, w_s5_gate, conv_dw_w, conv_dw_b, conv_ln_g, conv_ln_b, w_conv_out, w_out, norm_ffn_g, router_w, router_b, w_gate_up, b_gate_up, w_down, b_down, norm_final_g):
    assert norm_mix_g.shape[0] == 1, "single-layer block"
    bsz, seq, d_model = x.shape
    h1 = _mixer(x, meta_tokens, norm_mix_g[0], w_in[0], ssm_a_re[0], ssm_a_im[0], ssm_log_dt[0], ssm_b_re[0],
                ssm_b_im[0], ssm_c_re[0], ssm_c_im[0], ssm_d[0], w_s5_lin[0], w_s5_gate[0], conv_dw_w[0],
                conv_dw_b[0], conv_ln_g[0], conv_ln_b[0], w_conv_out[0], w_out[0], t_chunk=32)
    out = _moe_and_final_norm(h1.reshape(bsz * seq, d_model), norm_ffn_g[0], router_w[0], router_b[0],
                              w_gate_up[0], b_gate_up[0], w_down[0], b_down[0], norm_final_g)
    return out.reshape(bsz, seq, d_model).astype(x.dtype)
```

```python
import functools
import math

import jax
import jax.numpy as jnp
from jax import lax
from jax.experimental import pallas as pl
from jax.experimental.pallas import tpu as pltpu

RMS_EPS = 1e-6
LN_EPS = 1e-5
TOP_K = 4
SWIGLU_LIMIT = 7.0
SWIGLU_ALPHA = 1.702

LANES = 128
SUBLANES = 8
VMEM_LIMIT = 56 * 1024 * 1024
SCAN_CHAINS = 8
CONV_OUT_TILE = 8


def _sigmoid(v):
    return 1.0 / (1.0 + jnp.exp(-v))


def _const_spec(shape):
    nd = len(shape)
    return pl.BlockSpec(shape, lambda *_: (0,) * nd)


def _mixer_kernel(x_ref, meta_ref, g_ref, win_ref, ar_ref, ai_ref, bd_ref, ct_ref, dskip_ref,
                  wlin_ref, wgate_ref, dww_ref, dwb_ref, lng_ref, lnb_ref, wco_ref, wout_ref,
                  out_ref,
                  xs_ref, st_ref, state_ref, cbuf_ref, conv_ref,
                  *, t_chunk, d_ssm, d_conv, conv_w, sb_lanes):
    i = pl.program_id(0)
    bsz = SUBLANES
    rows = t_chunk * bsz
    d_model = xs_ref.shape[1]
    n_sb = d_ssm // LANES
    hist = (conv_w - 1) * bsz

    @pl.when(i == 0)
    def _():
        state_ref[...] = jnp.zeros_like(state_ref)
        cbuf_ref[...] = jnp.zeros_like(cbuf_ref)
        for t in range(t_chunk):
            xs_ref[t * bsz:(t + 1) * bsz, :] = jnp.broadcast_to(meta_ref[t:t + 1, :], (bsz, d_model))

    @pl.when(i > 0)
    def _():
        for t in range(t_chunk):
            xs_ref[t * bsz:(t + 1) * bsz, :] = x_ref[:, pl.ds(t, 1), :].reshape(bsz, d_model)

    xs = xs_ref[...]
    hn = xs * lax.rsqrt(jnp.mean(xs * xs, axis=-1, keepdims=True) + RMS_EPS) * g_ref[...]
    hn_bf = hn.astype(jnp.bfloat16)
    s1 = d_ssm
    s2 = s1 + 2 * d_conv
    in_proj = lambda lo, hi: jnp.dot(hn_bf, win_ref[:, lo:hi], preferred_element_type=jnp.float32)

    zc = in_proj(s1, s2)
    cbuf_ref[hist:hist + rows, :] = zc[:, :d_conv] * _sigmoid(zc[:, d_conv:])

    u = in_proj(0, s1)
    u_bf = u.astype(jnp.bfloat16)
    for sb in range(n_sb):
        st_ref[:, sb * 2 * sb_lanes:(sb + 1) * 2 * sb_lanes] = jnp.dot(
            u_bf[:, sb * LANES:(sb + 1) * LANES], bd_ref[sb], preferred_element_type=jnp.float32)

    for lb in range(d_conv // LANES):
        ls = slice(lb * LANES, (lb + 1) * LANES)
        wk = [jnp.broadcast_to(dww_ref[k:k + 1, ls], (bsz, LANES)) for k in range(conv_w)]
        for t0 in range(0, t_chunk, CONV_OUT_TILE):
            acc = [None] * CONV_OUT_TILE
            for tt in range(t0, t0 + CONV_OUT_TILE + conv_w - 1):
                v = cbuf_ref[tt * bsz:(tt + 1) * bsz, ls]
                for o in range(CONV_OUT_TILE):
                    k = tt - (t0 + o)
                    if 0 <= k < conv_w:
                        acc[o] = wk[k] * v if acc[o] is None else acc[o] + wk[k] * v
            for o in range(CONV_OUT_TILE):
                conv_ref[(t0 + o) * bsz:(t0 + o + 1) * bsz, ls] = acc[o]
    cbuf_ref[0:hist, :] = cbuf_ref[rows:rows + hist, :]

    zg = in_proj(s2, win_ref.shape[1])

    n_lb = sb_lanes // LANES
    blocks = []
    for sb in range(n_sb):
        base = sb * 2 * sb_lanes
        for j in range(n_lb):
            blocks.append((slice(base + j * LANES, base + (j + 1) * LANES),
                           slice(base + sb_lanes + j * LANES, base + sb_lanes + (j + 1) * LANES),
                           slice(sb * sb_lanes + j * LANES, sb * sb_lanes + (j + 1) * LANES)))
    for g0 in range(0, len(blocks), SCAN_CHAINS):
        grp = blocks[g0:g0 + SCAN_CHAINS]
        ar = [jnp.broadcast_to(ar_ref[:, la], (bsz, LANES)) for _, _, la in grp]
        ai = [jnp.broadcast_to(ai_ref[:, la], (bsz, LANES)) for _, _, la in grp]
        sre = [state_ref[:, lre] for lre, _, _ in grp]
        sim = [state_ref[:, lim] for _, lim, _ in grp]
        for t in range(t_chunk):
            r = slice(t * bsz, (t + 1) * bsz)
            for c, (lre, lim, _) in enumerate(grp):
                nre = ar[c] * sre[c] - ai[c] * sim[c] + st_ref[r, lre]
                nim = ar[c] * sim[c] + ai[c] * sre[c] + st_ref[r, lim]
                st_ref[r, lre] = nre
                st_ref[r, lim] = nim
                sre[c], sim[c] = nre, nim
        for c, (lre, lim, _) in enumerate(grp):
            state_ref[:, lre] = sre[c]
            state_ref[:, lim] = sim[c]
    ys = []
    for sb in range(n_sb):
        ys.append(jnp.dot(st_ref[:, sb * 2 * sb_lanes:(sb + 1) * 2 * sb_lanes].astype(jnp.bfloat16),
                          ct_ref[sb], preferred_element_type=jnp.float32))
    y = jnp.concatenate(ys, axis=-1) if n_sb > 1 else ys[0]
    y = y + dskip_ref[...] * u
    ya = 0.5 * y * (1.0 + lax.erf(y * (1.0 / math.sqrt(2.0))))
    ya_bf = ya.astype(jnp.bfloat16)
    ya2 = (jnp.dot(ya_bf, wlin_ref[...], preferred_element_type=jnp.float32)
           * _sigmoid(jnp.dot(ya_bf, wgate_ref[...], preferred_element_type=jnp.float32)))

    c = conv_ref[...] + dwb_ref[...]
    mu = jnp.mean(c, axis=-1, keepdims=True)
    cc = c - mu
    var = jnp.mean(cc * cc, axis=-1, keepdims=True)
    c = cc * lax.rsqrt(var + LN_EPS) * lng_ref[...] + lnb_ref[...]
    c = c * _sigmoid(c)
    yb = jnp.dot(c.astype(jnp.bfloat16), wco_ref[...], preferred_element_type=jnp.float32)

    merged = _sigmoid(zg[:, :d_model]) * ya2 + _sigmoid(zg[:, d_model:]) * yb
    h1 = xs + jnp.dot(merged.astype(jnp.bfloat16), wout_ref[...], preferred_element_type=jnp.float32)
    xs_ref[...] = h1
    for t in range(t_chunk):
        out_ref[:, pl.ds(t, 1), :] = xs_ref[t * bsz:(t + 1) * bsz, :].reshape(bsz, 1, d_model)


def _s5_discretize(a_re, a_im, log_dt, b_re, b_im, c_re, c_im):
    f32 = jnp.float32
    g, p = a_re.shape
    h = b_re.shape[-1]
    gps = LANES // h
    n_sb = g // gps
    dt = jnp.exp(log_dt.astype(f32))[:, None]
    decay = jnp.exp(a_re * dt)
    abar_re = decay * jnp.cos(a_im * dt)
    abar_im = decay * jnp.sin(a_im * dt)
    den = a_re * a_re + a_im * a_im
    nr = abar_re - 1.0
    coef_re = (nr * a_re + abar_im * a_im) / den
    coef_im = (abar_im * a_re - nr * a_im) / den
    bd_re = coef_re[..., None] * b_re - coef_im[..., None] * b_im
    bd_im = coef_re[..., None] * b_im + coef_im[..., None] * b_re
    eye = jnp.eye(gps, dtype=f32)

    def blockdiag_in(w):
        w = w.reshape(n_sb, gps, p, h)
        return jnp.einsum('sgph,gk->sghkp', w, eye).reshape(n_sb, gps * h, gps * p)

    def blockdiag_out(w):
        w = w.reshape(n_sb, gps, h, p)
        return jnp.einsum('sghp,gk->skpgh', w, eye).reshape(n_sb, gps * p, gps * h)

    bd = jnp.concatenate([blockdiag_in(bd_re), blockdiag_in(bd_im)], axis=-1)
    ct = jnp.concatenate([blockdiag_out(c_re), -blockdiag_out(c_im)], axis=1)
    return (abar_re.reshape(1, g * p), abar_im.reshape(1, g * p),
            bd.astype(jnp.bfloat16), ct.astype(jnp.bfloat16), gps * p)


def _mixer(x, meta_tokens, norm_g, w_in, a_re, a_im, log_dt, b_re, b_im, c_re, c_im, d_skip,
           w_s5_lin, w_s5_gate, dw_w, dw_b, ln_g, ln_b, w_conv_out, w_out, *, t_chunk):
    bsz, seq, d_model = x.shape
    n_meta = meta_tokens.shape[0]
    g, p = a_re.shape
    d_ssm = g * b_re.shape[-1]
    conv_w, _, d_conv = dw_w.shape
    assert bsz == SUBLANES and seq % t_chunk == 0 and t_chunk >= n_meta and t_chunk % SUBLANES == 0
    assert d_ssm % LANES == 0 and d_conv % LANES == 0
    f32, bf16 = jnp.float32, jnp.bfloat16
    ar, ai, bd, ct, sb_lanes = _s5_discretize(a_re.astype(f32), a_im.astype(f32), log_dt, b_re.astype(f32),
                                              b_im.astype(f32), c_re.astype(f32), c_im.astype(f32))
    meta_chunk = jnp.concatenate([jnp.zeros((t_chunk - n_meta, d_model), f32), meta_tokens.astype(f32)], axis=0)
    rows = t_chunk * bsz
    n_state = g * p
    consts = [
        meta_chunk, norm_g.reshape(1, d_model).astype(f32), w_in.astype(bf16), ar, ai, bd, ct,
        d_skip.reshape(1, d_ssm).astype(f32), w_s5_lin.astype(bf16), w_s5_gate.astype(bf16),
        dw_w.reshape(conv_w, d_conv).astype(f32), dw_b.reshape(1, d_conv).astype(f32),
        ln_g.reshape(1, d_conv).astype(f32), ln_b.reshape(1, d_conv).astype(f32),
        w_conv_out.astype(bf16), w_out.astype(bf16),
    ]
    x_spec = pl.BlockSpec((bsz, t_chunk, d_model), lambda i: (0, jnp.maximum(i - 1, 0), 0))
    kern = functools.partial(_mixer_kernel, t_chunk=t_chunk, d_ssm=d_ssm, d_conv=d_conv, conv_w=conv_w,
                             sb_lanes=sb_lanes)
    return pl.pallas_call(
        kern,
        out_shape=jax.ShapeDtypeStruct((bsz, seq, d_model), f32),
        grid=(seq // t_chunk + 1,),
        in_specs=[x_spec] + [_const_spec(c.shape) for c in consts],
        out_specs=x_spec,
        scratch_shapes=[
            pltpu.VMEM((rows, d_model), f32),
            pltpu.VMEM((rows, 2 * n_state), f32),
            pltpu.VMEM((bsz, 2 * n_state), f32),
            pltpu.VMEM(((conv_w - 1) * bsz + rows, d_conv), f32),
            pltpu.VMEM((rows, d_conv), f32),
        ],
        compiler_params=pltpu.CompilerParams(dimension_semantics=("arbitrary",), vmem_limit_bytes=VMEM_LIMIT),
        name="mixer",
    )(x.astype(f32), *consts)


ROW_PACK = 2
RUN_CHUNK = 16


def _router_kernel(h_ref, g_ref, rwh_ref, rwl_ref, rb_ref, xb_ref, lp_ref, prob_ref, cnt_ref, *, n_experts):
    tm = h_ref.shape[0]
    h = h_ref[...]
    hn = h * lax.rsqrt(jnp.mean(h * h, axis=-1, keepdims=True) + RMS_EPS) * g_ref[...]
    hn_hi = hn.astype(jnp.bfloat16)
    xb_ref[...] = hn_hi
    hn_lo = (hn - hn_hi.astype(jnp.float32)).astype(jnp.bfloat16)
    nt = (((1,), (1,)), ((), ()))
    dot = lambda a, b: lax.dot_general(a, b, nt, preferred_element_type=jnp.float32)
    lg = dot(rwh_ref[...], hn_hi) + dot(rwh_ref[...], hn_lo) + dot(rwl_ref[...], hn_hi) + rb_ref[...]
    eidx = lax.broadcasted_iota(jnp.int32, (n_experts, tm), 0)
    work = lg
    vals, hots = [], []
    for _ in range(TOP_K):
        m = jnp.max(work, axis=0, keepdims=True)
        sel = jnp.min(jnp.where(work == m, eidx, n_experts), axis=0, keepdims=True)
        hot = eidx == sel
        vals.append(m)
        hots.append(hot)
        work = jnp.where(hot, -jnp.inf, work)
    exps = [jnp.exp(v - vals[0]) for v in vals]
    tot = exps[0]
    for e in exps[1:]:
        tot = tot + e
    inv = 1.0 / tot
    oh = hots[0].astype(jnp.float32)
    for hot in hots[1:]:
        oh = oh + hot.astype(jnp.float32)
    upper = (lax.broadcasted_iota(jnp.int32, (tm, tm), 0) < lax.broadcasted_iota(jnp.int32, (tm, tm), 1))
    cum = jnp.dot(oh.astype(jnp.bfloat16), upper.astype(jnp.bfloat16), preferred_element_type=jnp.float32)
    cnt = jnp.sum(oh, axis=1, keepdims=True)
    lower = (lax.broadcasted_iota(jnp.int32, (n_experts, n_experts), 1)
             < lax.broadcasted_iota(jnp.int32, (n_experts, n_experts), 0))
    cnt_b = jnp.broadcast_to(cnt, (n_experts, LANES))
    pairs = jnp.floor((cnt_b + 1.0) * 0.5)
    cnt_b = 2.0 * pairs
    c_hi = jnp.floor(cnt_b * (1.0 / 16.0))
    c_lo = cnt_b - 16.0 * c_hi
    ldot = lambda v: jnp.dot(lower.astype(jnp.bfloat16), v.astype(jnp.bfloat16), preferred_element_type=jnp.float32)
    loff = 16.0 * ldot(c_hi) + ldot(c_lo)
    base = cum + loff[:, 0:1]
    for k in range(TOP_K):
        prob_ref[k:k + 1, :] = exps[k] * inv
        lp_ref[k:k + 1, :] = jnp.sum(jnp.where(hots[k], base, 0.0), axis=0, keepdims=True).astype(jnp.int32)
    cnt_ref[...] = pairs.astype(jnp.int32)


def _router(h1, norm_g, router_w, router_b, *, tm):
    n, d = h1.shape
    e = router_w.shape[-1]
    f32 = jnp.float32
    rwt = router_w.astype(f32).T
    rwh = rwt.astype(jnp.bfloat16)
    rwl = (rwt - rwh.astype(f32)).astype(jnp.bfloat16)
    kn = lambda dt: jax.ShapeDtypeStruct((TOP_K, n), dt)
    kspec = pl.BlockSpec((TOP_K, tm), lambda i: (0, i))
    return pl.pallas_call(
        functools.partial(_router_kernel, n_experts=e),
        out_shape=(jax.ShapeDtypeStruct((n, d), jnp.bfloat16), kn(jnp.int32), kn(f32),
                   jax.ShapeDtypeStruct((n // tm, e, LANES), jnp.int32)),
        grid=(n // tm,),
        in_specs=[pl.BlockSpec((tm, d), lambda i: (i, 0)), _const_spec((1, d)), _const_spec((e, d)),
                  _const_spec((e, d)), _const_spec((e, 1))],
        out_specs=(pl.BlockSpec((tm, d), lambda i: (i, 0)), kspec, kspec,
                   pl.BlockSpec((None, e, LANES), lambda i: (i, 0, 0))),
        compiler_params=pltpu.CompilerParams(dimension_semantics=("arbitrary",), vmem_limit_bytes=VMEM_LIMIT),
        name="router",
    )(h1, norm_g.reshape(1, d).astype(f32), rwh, rwl, router_b.reshape(e, 1).astype(f32))


def _sort_matrix(lp, n_rows, axis):
    if axis == 0:
        shape = (n_rows, lp.shape[1])
        hit = lambda k: lax.broadcasted_iota(jnp.int32, shape, 0) == lp[k:k + 1, :]
    else:
        shape = (lp.shape[0], n_rows)
        hit = lambda k: lax.broadcasted_iota(jnp.int32, shape, 1) == lp[:, k:k + 1]
    m = hit(0)
    for k in range(1, TOP_K):
        m = m | hit(k)
    return m


def _start_runs(n_runs, run_args, src_ref, dst_ref, sem, src_step=RUN_CHUNK):
    def one_run(r, started):
        s, d, ln = run_args(r)
        n_ch = (jnp.maximum(ln, 0) + (RUN_CHUNK - 1)) // RUN_CHUNK

        def one_chunk(q, c):
            pltpu.make_async_copy(src_ref.at[pl.ds(s + q * src_step, RUN_CHUNK)],
                                  dst_ref.at[pl.ds(d + q * RUN_CHUNK, RUN_CHUNK)], sem).start()
            return c

        lax.fori_loop(0, n_ch, one_chunk, 0)
        return started + n_ch

    return lax.fori_loop(0, n_runs, one_run, jnp.int32(0))


def _drain(count, src_ref, dst_ref, sem):
    def wait_one(q, c):
        pltpu.make_async_copy(src_ref.at[pl.ds(0, RUN_CHUNK)], dst_ref.at[pl.ds(0, RUN_CHUNK)], sem).wait()
        return c

    lax.fori_loop(0, count, wait_one, 0)


def _dispatch_kernel(cnt_ref, loff_ref, dst_ref, fill_at_ref, fill_len_ref, xb_ref, lp_ref, xs_ref,
                     xl_ref, pend_ref, sem, *, n_experts):
    i = pl.program_id(0)
    rows = xl_ref.shape[1] - RUN_CHUNK
    d = xl_ref.shape[3]

    @pl.when(i == 0)
    def _():
        for s in range(2):
            xl_ref[s, rows:] = jnp.zeros((RUN_CHUNK, ROW_PACK, d), xl_ref.dtype)
        n_fill = _start_runs(n_experts + 1, lambda r: (rows, fill_at_ref[r], fill_len_ref[r]), xl_ref.at[0], xs_ref,
                             sem, src_step=0)
        _drain(n_fill, xl_ref.at[0], xs_ref, sem)
        pend_ref[0] = 0

    src = xl_ref.at[i % 2]
    perm = _sort_matrix(lp_ref[...], rows * ROW_PACK, 0).astype(jnp.bfloat16)
    sorted_rows = jnp.dot(perm, xb_ref[...], preferred_element_type=jnp.float32).astype(jnp.bfloat16)
    src[:rows] = sorted_rows.reshape(rows, ROW_PACK, d)

    def run_args(e):
        r = i * n_experts + e
        return loff_ref[r], dst_ref[r], cnt_ref[r]

    _drain(pend_ref[0], src, xs_ref, sem)
    pend_ref[0] = _start_runs(n_experts, run_args, src, xs_ref, sem)

    @pl.when(i == pl.num_programs(0) - 1)
    def _():
        _drain(pend_ref[0], src, xs_ref, sem)


def _dispatch(xb, lp, cnt_flat, loff_flat, dst_flat, fill_at, fill_len, m_pad, local_pairs, *, tm, n_experts):
    n, d = xb.shape
    grid_spec = pltpu.PrefetchScalarGridSpec(
        num_scalar_prefetch=5,
        grid=(n // tm,),
        in_specs=[pl.BlockSpec((tm, d), lambda i, *_: (i, 0)), pl.BlockSpec((TOP_K, tm), lambda i, *_: (0, i))],
        out_specs=pl.BlockSpec(memory_space=pl.ANY),
        scratch_shapes=[pltpu.VMEM((2, local_pairs + RUN_CHUNK, ROW_PACK, d), jnp.bfloat16),
                        pltpu.SMEM((1,), jnp.int32), pltpu.SemaphoreType.DMA],
    )
    return pl.pallas_call(
        functools.partial(_dispatch_kernel, n_experts=n_experts),
        out_shape=jax.ShapeDtypeStruct((m_pad, ROW_PACK, d), jnp.bfloat16),
        grid_spec=grid_spec,
        compiler_params=pltpu.CompilerParams(dimension_semantics=("arbitrary",), vmem_limit_bytes=VMEM_LIMIT),
        name="dispatch",
    )(cnt_flat, loff_flat, dst_flat, fill_at, fill_len, xb, lp)


def _expert_kernel(te_ref, nu_ref, tj_ref, ilo_ref, ihi_ref, cnt_ref, loff_ref, pre_ref, bfill_at_ref, bfill_len_ref,
                   xs_ref, wgu_ref, bgu_ref, wd_ref, bd_ref, ys_ref, yb_ref, pend_ref, sem,
                   *, n_experts, blk_rows):
    j = pl.program_id(0)
    tm, _, d = xs_ref.shape

    @pl.when(j == 0)
    def _():
        for s in range(2):
            yb_ref[s, tm:] = jnp.zeros((RUN_CHUNK, ROW_PACK, d), yb_ref.dtype)
        n_blocks = ys_ref.shape[0] // blk_rows
        n_fill = _start_runs(n_blocks, lambda r: (tm, bfill_at_ref[r], bfill_len_ref[r]), yb_ref.at[0],
                             ys_ref, sem, src_step=0)
        _drain(n_fill, yb_ref.at[0], ys_ref, sem)
        pend_ref[0] = 0

    src = yb_ref.at[j % 2]

    @pl.when(j < nu_ref[0])
    def _():
        d_exp = wd_ref.shape[0]
        x = xs_ref[...].reshape(tm * ROW_PACK, d)
        gu = jnp.dot(x, wgu_ref[...], preferred_element_type=jnp.float32) + bgu_ref[...]
        g = jnp.minimum(gu[:, :d_exp], SWIGLU_LIMIT)
        lin = jnp.clip(gu[:, d_exp:], -SWIGLU_LIMIT, SWIGLU_LIMIT)
        act = g * _sigmoid(SWIGLU_ALPHA * g) * (lin + 1.0)
        y = jnp.dot(act.astype(jnp.bfloat16), wd_ref[...], preferred_element_type=jnp.float32) + bd_ref[...]
        src[:tm] = y.astype(jnp.bfloat16).reshape(tm, ROW_PACK, d)

    _drain(pend_ref[0], src, ys_ref, sem)
    pend_ref[0] = 0

    @pl.when(j < nu_ref[0])
    def _():
        e = te_ref[j]
        base = tj_ref[j] * tm
        i0 = ilo_ref[j]

        def run_args(r):
            it = i0 + r
            a = pre_ref[it * n_experts + e] - base
            b = a + cnt_ref[it * n_experts + e]
            ac = jnp.maximum(a, 0)
            return ac, it * blk_rows + loff_ref[it * n_experts + e] + (ac - a), jnp.minimum(b, tm) - ac

        pend_ref[0] = _start_runs(ihi_ref[j] - i0, run_args, src, ys_ref, sem)

    @pl.when(j == pl.num_programs(0) - 1)
    def _():
        _drain(pend_ref[0], src, ys_ref, sem)


def _experts(xs, sched, w_gate_up, b_gate_up, w_down, b_down, n_tok_tiles, blk_rows, *, tm):
    m_pad = xs.shape[0]
    e, d, d_gu = w_gate_up.shape
    d_exp = w_down.shape[1]
    bf16, f32 = jnp.bfloat16, jnp.float32
    row_map = lambda j, te, nu, *_: (jnp.minimum(j, nu[0] - 1), 0, 0)
    w_map = lambda j, te, *_: (te[j], 0, 0)
    grid_spec = pltpu.PrefetchScalarGridSpec(
        num_scalar_prefetch=len(sched),
        grid=(m_pad // tm,),
        in_specs=[pl.BlockSpec((tm, ROW_PACK, d), row_map),
                  pl.BlockSpec((None, d, d_gu), w_map), pl.BlockSpec((None, 1, d_gu), w_map),
                  pl.BlockSpec((None, d_exp, d), w_map), pl.BlockSpec((None, 1, d), w_map)],
        out_specs=pl.BlockSpec(memory_space=pl.ANY),
        scratch_shapes=[pltpu.VMEM((2, tm + RUN_CHUNK, ROW_PACK, d), bf16),
                        pltpu.SMEM((1,), jnp.int32), pltpu.SemaphoreType.DMA],
    )
    return pl.pallas_call(
        functools.partial(_expert_kernel, n_experts=e, blk_rows=blk_rows),
        out_shape=jax.ShapeDtypeStruct((n_tok_tiles * blk_rows, ROW_PACK, d), bf16),
        grid_spec=grid_spec,
        compiler_params=pltpu.CompilerParams(dimension_semantics=("arbitrary",), vmem_limit_bytes=VMEM_LIMIT),
        name="experts",
    )(*sched, xs, w_gate_up.astype(bf16), b_gate_up.reshape(e, 1, d_gu).astype(f32),
      w_down.astype(bf16), b_down.reshape(e, 1, d).astype(f32))


def _combine_kernel(h_ref, ys_ref, lp_ref, lpt_ref, p_ref, g_ref, out_ref):
    tm, d = h_ref.shape
    pairs = ys_ref.shape[0] - RUN_CHUNK
    rows = pairs * ROW_PACK
    y = ys_ref[:pairs].reshape(rows, d).astype(jnp.float32)
    lp = lp_ref[...]
    riota = lax.broadcasted_iota(jnp.int32, (rows, tm), 0)
    wmat = jnp.where(riota == lp[0:1, :], p_ref[0:1, :], 0.0)
    for k in range(1, TOP_K):
        wmat = wmat + jnp.where(riota == lp[k:k + 1, :], p_ref[k:k + 1, :], 0.0)
    pw = jnp.sum(wmat, axis=1, keepdims=True)
    unsort = _sort_matrix(lpt_ref[...], rows, 1).astype(jnp.bfloat16)
    acc = h_ref[...] + jnp.dot(unsort, (y * pw).astype(jnp.bfloat16), preferred_element_type=jnp.float32)
    out_ref[...] = acc * lax.rsqrt(jnp.mean(acc * acc, axis=-1, keepdims=True) + RMS_EPS) * g_ref[...]


def _combine(h1, ys, lp, lpt, prob, norm_g, blk_rows, *, tm):
    n, d = h1.shape
    f32 = jnp.float32
    kspec = pl.BlockSpec((TOP_K, tm), lambda i: (0, i))
    return pl.pallas_call(
        _combine_kernel,
        out_shape=jax.ShapeDtypeStruct((n, d), f32),
        grid=(n // tm,),
        in_specs=[pl.BlockSpec((tm, d), lambda i: (i, 0)),
                  pl.BlockSpec((blk_rows, ROW_PACK, d), lambda i: (i, 0, 0)),
                  kspec, pl.BlockSpec((tm, TOP_K), lambda i: (i, 0)), kspec, _const_spec((1, d))],
        out_specs=pl.BlockSpec((tm, d), lambda i: (i, 0)),
        compiler_params=pltpu.CompilerParams(dimension_semantics=("arbitrary",), vmem_limit_bytes=VMEM_LIMIT),
        name="combine",
    )(h1, ys, lp, lpt, prob, norm_g.reshape(1, d).astype(f32))


def _tile_rows(n, target):
    t = min(target, n)
    while n % t:
        t //= 2
    return t


def _moe_schedule(cnt, tm_tok, tm_e, blk_rows):
    n_t, e = cnt.shape
    i32 = jnp.int32
    total = jnp.sum(cnt, axis=0)
    tiles_per_e = (total + RUN_CHUNK + tm_e - 1) // tm_e
    tile_end = jnp.cumsum(tiles_per_e)
    tile_begin = tile_end - tiles_per_e
    n_used = tile_end[-1:]
    n_tiles = ((TOP_K * tm_tok + e) * n_t) // (ROW_PACK * tm_e) + 2 * e + 1
    tile_id = jnp.minimum(jnp.arange(n_tiles, dtype=i32), n_used[0] - 1)
    tile_expert = jnp.sum(tile_id[:, None] >= tile_end[None, :], axis=1).astype(i32)
    tile_j = tile_id - tile_begin[tile_expert]
    pre = jnp.cumsum(cnt, axis=0) - cnt
    loff = jnp.cumsum(cnt, axis=1) - cnt
    dst = tile_begin[None, :] * tm_e + pre
    pre_t = pre[:, tile_expert].T
    end_t = pre_t + cnt[:, tile_expert].T
    ilo = jnp.sum(end_t <= (tile_j * tm_e)[:, None], axis=1).astype(i32)
    ihi = jnp.sum(pre_t < ((tile_j + 1) * tm_e)[:, None], axis=1).astype(i32)
    gap = tiles_per_e * tm_e - total
    fill_len = (gap + RUN_CHUNK - 1) // RUN_CHUNK * RUN_CHUNK
    fill_at = jnp.concatenate([tile_end * tm_e - fill_len, n_used * tm_e])
    fill_len = jnp.concatenate([fill_len, (n_tiles - n_used) * tm_e])
    bfill_len = (blk_rows - jnp.sum(cnt, axis=1) + RUN_CHUNK - 1) // RUN_CHUNK * RUN_CHUNK
    bfill_at = (jnp.arange(n_t, dtype=i32) + 1) * blk_rows - bfill_len
    flat = lambda a: a.reshape(-1).astype(i32)
    return dict(n_tiles=n_tiles, tile_expert=tile_expert, n_used=n_used.astype(i32), tile_j=tile_j.astype(i32),
                ilo=ilo, ihi=ihi, cnt=flat(cnt), loff=flat(loff), pre=flat(pre), dst=flat(dst),
                fill_at=flat(fill_at), fill_len=flat(fill_len), bfill_at=flat(bfill_at), bfill_len=flat(bfill_len))


def _moe_and_final_norm(h1, norm_ffn_g, router_w, router_b, w_gate_up, b_gate_up, w_down, b_down, norm_final_g):
    n, d = h1.shape
    e = router_w.shape[-1]
    tm_tok = _tile_rows(n, 256)
    tm_e = _tile_rows(n, 512) // ROW_PACK
    local_rows = -(-(TOP_K * tm_tok + e) // (ROW_PACK * RUN_CHUNK)) * (ROW_PACK * RUN_CHUNK)
    blk_rows = local_rows // ROW_PACK + RUN_CHUNK
    xb, lp, prob, cnt3 = _router(h1, norm_ffn_g, router_w, router_b, tm=tm_tok)
    s = _moe_schedule(cnt3[:, :, 0], tm_tok, tm_e, blk_rows)
    xs = _dispatch(xb, lp, s["cnt"], s["loff"], s["dst"], s["fill_at"], s["fill_len"], s["n_tiles"] * tm_e,
                   local_rows // ROW_PACK, tm=tm_tok, n_experts=e)
    sched = (s["tile_expert"], s["n_used"], s["tile_j"], s["ilo"], s["ihi"], s["cnt"], s["loff"], s["pre"],
             s["bfill_at"], s["bfill_len"])
    ys = _experts(xs, sched, w_gate_up, b_gate_up, w_down, b_down, n // tm_tok, blk_rows, tm=tm_e)
    return _combine(h1, ys, lp, lp.T, prob, norm_final_g, blk_rows, tm=tm_tok)


def kernel(x, meta_tokens, norm_mix_g, w_in, ssm_a_re, ssm_a_im, ssm_log_dt, ssm_b_re, ssm_b_im, ssm_c_re, ssm_c_im, ssm_d, w_s5_lin, w_s5_gate, conv_dw_w, conv_dw_b, conv_ln_g, conv_ln_b, w_conv_out, w_out, norm_ffn_g, router_w, router_b, w_gate_up, b_gate_up, w_down, b_down, norm_final_g):
    assert norm_mix_g.shape[0] == 1, "single-layer block"
    bsz, seq, d_model = x.shape
    h1 = _mixer(x, meta_tokens, norm_mix_g[0], w_in[0], ssm_a_re[0], ssm_a_im[0], ssm_log_dt[0], ssm_b_re[0],
                ssm_b_im[0], ssm_c_re[0], ssm_c_im[0], ssm_d[0], w_s5_lin[0], w_s5_gate[0], conv_dw_w[0],
                conv_dw_b[0], conv_ln_g[0], conv_ln_b[0], w_conv_out[0], w_out[0], t_chunk=32)
    out = _moe_and_final_norm(h1.reshape(bsz * seq, d_model), norm_ffn_g[0], router_w[0], router_b[0],
                              w_gate_up[0], b_gate_up[0], w_down[0], b_down[0], norm_final_g)
    return out.reshape(bsz, seq, d_model).astype(x.dtype)
```

```python
import functools
import math

import jax
import jax.numpy as jnp
from jax import lax
from jax.experimental import pallas as pl
from jax.experimental.pallas import tpu as pltpu

RMS_EPS = 1e-6
LN_EPS = 1e-5
TOP_K = 4
SWIGLU_LIMIT = 7.0
SWIGLU_ALPHA = 1.702

LANES = 128
SUBLANES = 8
VMEM_LIMIT = 56 * 1024 * 1024
SCAN_CHAINS = 8
CONV_OUT_TILE = 8


def _sigmoid(v):
    return 1.0 / (1.0 + jnp.exp(-v))


def _const_spec(shape):
    nd = len(shape)
    return pl.BlockSpec(shape, lambda *_: (0,) * nd)


def _mixer_kernel(x_ref, meta_ref, g_ref, win_ref, ar_ref, ai_ref, bd_ref, ct_ref, dskip_ref,
                  wlin_ref, wgate_ref, dww_ref, dwb_ref, lng_ref, lnb_ref, wco_ref, wout_ref,
                  out_ref,
                  xs_ref, st_ref, state_ref, cbuf_ref, conv_ref,
                  *, t_chunk, d_ssm, d_conv, conv_w, sb_lanes):
    i = pl.program_id(0)
    bsz = SUBLANES
    rows = t_chunk * bsz
    d_model = xs_ref.shape[1]
    n_sb = d_ssm // LANES
    hist = (conv_w - 1) * bsz

    @pl.when(i == 0)
    def _():
        state_ref[...] = jnp.zeros_like(state_ref)
        cbuf_ref[...] = jnp.zeros_like(cbuf_ref)
        for t in range(t_chunk):
            xs_ref[t * bsz:(t + 1) * bsz, :] = jnp.broadcast_to(meta_ref[t:t + 1, :], (bsz, d_model))

    @pl.when(i > 0)
    def _():
        for t in range(t_chunk):
            xs_ref[t * bsz:(t + 1) * bsz, :] = x_ref[:, pl.ds(t, 1), :].reshape(bsz, d_model)

    xs = xs_ref[...]
    hn = xs * lax.rsqrt(jnp.mean(xs * xs, axis=-1, keepdims=True) + RMS_EPS) * g_ref[...]
    hn_bf = hn.astype(jnp.bfloat16)
    s1 = d_ssm
    s2 = s1 + 2 * d_conv
    in_proj = lambda lo, hi: jnp.dot(hn_bf, win_ref[:, lo:hi], preferred_element_type=jnp.float32)

    zc = in_proj(s1, s2)
    cbuf_ref[hist:hist + rows, :] = zc[:, :d_conv] * _sigmoid(zc[:, d_conv:])

    u = in_proj(0, s1)
    u_bf = u.astype(jnp.bfloat16)
    for sb in range(n_sb):
        st_ref[:, sb * 2 * sb_lanes:(sb + 1) * 2 * sb_lanes] = jnp.dot(
            u_bf[:, sb * LANES:(sb + 1) * LANES], bd_ref[sb], preferred_element_type=jnp.float32)

    for lb in range(d_conv // LANES):
        ls = slice(lb * LANES, (lb + 1) * LANES)
        wk = [jnp.broadcast_to(dww_ref[k:k + 1, ls], (bsz, LANES)) for k in range(conv_w)]
        for t0 in range(0, t_chunk, CONV_OUT_TILE):
            acc = [None] * CONV_OUT_TILE
            for tt in range(t0, t0 + CONV_OUT_TILE + conv_w - 1):
                v = cbuf_ref[tt * bsz:(tt + 1) * bsz, ls]
                for o in range(CONV_OUT_TILE):
                    k = tt - (t0 + o)
                    if 0 <= k < conv_w:
                        acc[o] = wk[k] * v if acc[o] is None else acc[o] + wk[k] * v
            for o in range(CONV_OUT_TILE):
                conv_ref[(t0 + o) * bsz:(t0 + o + 1) * bsz, ls] = acc[o]
    cbuf_ref[0:hist, :] = cbuf_ref[rows:rows + hist, :]

    zg = in_proj(s2, win_ref.shape[1])

    n_lb = sb_lanes // LANES
    blocks = []
    for sb in range(n_sb):
        base = sb * 2 * sb_lanes
        for j in range(n_lb):
            blocks.append((slice(base + j * LANES, base + (j + 1) * LANES),
                           slice(base + sb_lanes + j * LANES, base + sb_lanes + (j + 1) * LANES),
                           slice(sb * sb_lanes + j * LANES, sb * sb_lanes + (j + 1) * LANES)))
    for g0 in range(0, len(blocks), SCAN_CHAINS):
        grp = blocks[g0:g0 + SCAN_CHAINS]
        ar = [jnp.broadcast_to(ar_ref[:, la], (bsz, LANES)) for _, _, la in grp]
        ai = [jnp.broadcast_to(ai_ref[:, la], (bsz, LANES)) for _, _, la in grp]
        sre = [state_ref[:, lre] for lre, _, _ in grp]
        sim = [state_ref[:, lim] for _, lim, _ in grp]
        for t in range(t_chunk):
            r = slice(t * bsz, (t + 1) * bsz)
            for c, (lre, lim, _) in enumerate(grp):
                nre = ar[c] * sre[c] - ai[c] * sim[c] + st_ref[r, lre]
                nim = ar[c] * sim[c] + ai[c] * sre[c] + st_ref[r, lim]
                st_ref[r, lre] = nre
                st_ref[r, lim] = nim
                sre[c], sim[c] = nre, nim
        for c, (lre, lim, _) in enumerate(grp):
            state_ref[:, lre] = sre[c]
            state_ref[:, lim] = sim[c]
    ys = []
    for sb in range(n_sb):
        ys.append(jnp.dot(st_ref[:, sb * 2 * sb_lanes:(sb + 1) * 2 * sb_lanes].astype(jnp.bfloat16),
                          ct_ref[sb], preferred_element_type=jnp.float32))
    y = jnp.concatenate(ys, axis=-1) if n_sb > 1 else ys[0]
    y = y + dskip_ref[...] * u
    ya = 0.5 * y * (1.0 + lax.erf(y * (1.0 / math.sqrt(2.0))))
    ya_bf = ya.astype(jnp.bfloat16)
    ya2 = (jnp.dot(ya_bf, wlin_ref[...], preferred_element_type=jnp.float32)
           * _sigmoid(jnp.dot(ya_bf, wgate_ref[...], preferred_element_type=jnp.float32)))

    c = conv_ref[...] + dwb_ref[...]
    mu = jnp.mean(c, axis=-1, keepdims=True)
    cc = c - mu
    var = jnp.mean(cc * cc, axis=-1, keepdims=True)
    c = cc * lax.rsqrt(var + LN_EPS) * lng_ref[...] + lnb_ref[...]
    c = c * _sigmoid(c)
    yb = jnp.dot(c.astype(jnp.bfloat16), wco_ref[...], preferred_element_type=jnp.float32)

    merged = _sigmoid(zg[:, :d_model]) * ya2 + _sigmoid(zg[:, d_model:]) * yb
    h1 = xs + jnp.dot(merged.astype(jnp.bfloat16), wout_ref[...], preferred_element_type=jnp.float32)
    xs_ref[...] = h1
    for t in range(t_chunk):
        out_ref[:, pl.ds(t, 1), :] = xs_ref[t * bsz:(t + 1) * bsz, :].reshape(bsz, 1, d_model)


def _s5_discretize(a_re, a_im, log_dt, b_re, b_im, c_re, c_im):
    f32 = jnp.float32
    g, p = a_re.shape
    h = b_re.shape[-1]
    gps = LANES // h
    n_sb = g // gps
    dt = jnp.exp(log_dt.astype(f32))[:, None]
    decay = jnp.exp(a_re * dt)
    abar_re = decay * jnp.cos(a_im * dt)
    abar_im = decay * jnp.sin(a_im * dt)
    den = a_re * a_re + a_im * a_im
    nr = abar_re - 1.0
    coef_re = (nr * a_re + abar_im * a_im) / den
    coef_im = (abar_im * a_re - nr * a_im) / den
    bd_re = coef_re[..., None] * b_re - coef_im[..., None] * b_im
    bd_im = coef_re[..., None] * b_im + coef_im[..., None] * b_re
    eye = jnp.eye(gps, dtype=f32)

    def blockdiag_in(w):
        w = w.reshape(n_sb, gps, p, h)
        return jnp.einsum('sgph,gk->sghkp', w, eye).reshape(n_sb, gps * h, gps * p)

    def blockdiag_out(w):
        w = w.reshape(n_sb, gps, h, p)
        return jnp.einsum('sghp,gk->skpgh', w, eye).reshape(n_sb, gps * p, gps * h)

    bd = jnp.concatenate([blockdiag_in(bd_re), blockdiag_in(bd_im)], axis=-1)
    ct = jnp.concatenate([blockdiag_out(c_re), -blockdiag_out(c_im)], axis=1)
    return (abar_re.reshape(1, g * p), abar_im.reshape(1, g * p),
            bd.astype(jnp.bfloat16), ct.astype(jnp.bfloat16), gps * p)


def _mixer(x, meta_tokens, norm_g, w_in, a_re, a_im, log_dt, b_re, b_im, c_re, c_im, d_skip,
           w_s5_lin, w_s5_gate, dw_w, dw_b, ln_g, ln_b, w_conv_out, w_out, *, t_chunk):
    bsz, seq, d_model = x.shape
    n_meta = meta_tokens.shape[0]
    g, p = a_re.shape
    d_ssm = g * b_re.shape[-1]
    conv_w, _, d_conv = dw_w.shape
    assert bsz == SUBLANES and seq % t_chunk == 0 and t_chunk >= n_meta and t_chunk % SUBLANES == 0
    assert d_ssm % LANES == 0 and d_conv % LANES == 0
    f32, bf16 = jnp.float32, jnp.bfloat16
    ar, ai, bd, ct, sb_lanes = _s5_discretize(a_re.astype(f32), a_im.astype(f32), log_dt, b_re.astype(f32),
                                              b_im.astype(f32), c_re.astype(f32), c_im.astype(f32))
    meta_chunk = jnp.concatenate([jnp.zeros((t_chunk - n_meta, d_model), f32), meta_tokens.astype(f32)], axis=0)
    rows = t_chunk * bsz
    n_state = g * p
    consts = [
        meta_chunk, norm_g.reshape(1, d_model).astype(f32), w_in.astype(bf16), ar, ai, bd, ct,
        d_skip.reshape(1, d_ssm).astype(f32), w_s5_lin.astype(bf16), w_s5_gate.astype(bf16),
        dw_w.reshape(conv_w, d_conv).astype(f32), dw_b.reshape(1, d_conv).astype(f32),
        ln_g.reshape(1, d_conv).astype(f32), ln_b.reshape(1, d_conv).astype(f32),
        w_conv_out.astype(bf16), w_out.astype(bf16),
    ]
    x_spec = pl.BlockSpec((bsz, t_chunk, d_model), lambda i: (0, jnp.maximum(i - 1, 0), 0))
    kern = functools.partial(_mixer_kernel, t_chunk=t_chunk, d_ssm=d_ssm, d_conv=d_conv, conv_w=conv_w,
                             sb_lanes=sb_lanes)
    return pl.pallas_call(
        kern,
        out_shape=jax.ShapeDtypeStruct((bsz, seq, d_model), f32),
        grid=(seq // t_chunk + 1,),
        in_specs=[x_spec] + [_const_spec(c.shape) for c in consts],
        out_specs=x_spec,
        scratch_shapes=[
            pltpu.VMEM((rows, d_model), f32),
            pltpu.VMEM((rows, 2 * n_state), f32),
            pltpu.VMEM((bsz, 2 * n_state), f32),
            pltpu.VMEM(((conv_w - 1) * bsz + rows, d_conv), f32),
            pltpu.VMEM((rows, d_conv), f32),
        ],
        compiler_params=pltpu.CompilerParams(dimension_semantics=("arbitrary",), vmem_limit_bytes=VMEM_LIMIT),
        name="mixer",
    )(x.astype(f32), *consts)


ROW_PACK = 2
RUN_CHUNK = 32
WAIT_GROUP = 8


def _router_kernel(h_ref, g_ref, rwh_ref, rwl_ref, rb_ref, xb_ref, lp_ref, prob_ref, cnt_ref, *, n_experts):
    tm = h_ref.shape[0]
    h = h_ref[...]
    hn = h * lax.rsqrt(jnp.mean(h * h, axis=-1, keepdims=True) + RMS_EPS) * g_ref[...]
    hn_hi = hn.astype(jnp.bfloat16)
    xb_ref[...] = hn_hi
    hn_lo = (hn - hn_hi.astype(jnp.float32)).astype(jnp.bfloat16)
    nt = (((1,), (1,)), ((), ()))
    dot = lambda a, b: lax.dot_general(a, b, nt, preferred_element_type=jnp.float32)
    lg = dot(rwh_ref[...], hn_hi) + dot(rwh_ref[...], hn_lo) + dot(rwl_ref[...], hn_hi) + rb_ref[...]
    eidx = lax.broadcasted_iota(jnp.int32, (n_experts, tm), 0)
    work = lg
    vals, hots = [], []
    for _ in range(TOP_K):
        m = jnp.max(work, axis=0, keepdims=True)
        sel = jnp.min(jnp.where(work == m, eidx, n_experts), axis=0, keepdims=True)
        hot = eidx == sel
        vals.append(m)
        hots.append(hot)
        work = jnp.where(hot, -jnp.inf, work)
    exps = [jnp.exp(v - vals[0]) for v in vals]
    tot = exps[0]
    for e in exps[1:]:
        tot = tot + e
    inv = 1.0 / tot
    oh = hots[0].astype(jnp.float32)
    for hot in hots[1:]:
        oh = oh + hot.astype(jnp.float32)
    upper = (lax.broadcasted_iota(jnp.int32, (tm, tm), 0) < lax.broadcasted_iota(jnp.int32, (tm, tm), 1))
    cum = jnp.dot(oh.astype(jnp.bfloat16), upper.astype(jnp.bfloat16), preferred_element_type=jnp.float32)
    cnt = jnp.sum(oh, axis=1, keepdims=True)
    lower = (lax.broadcasted_iota(jnp.int32, (n_experts, n_experts), 1)
             < lax.broadcasted_iota(jnp.int32, (n_experts, n_experts), 0))
    cnt_b = jnp.broadcast_to(cnt, (n_experts, LANES))
    pairs = jnp.floor((cnt_b + 1.0) * 0.5)
    cnt_b = 2.0 * pairs
    c_hi = jnp.floor(cnt_b * (1.0 / 16.0))
    c_lo = cnt_b - 16.0 * c_hi
    ldot = lambda v: jnp.dot(lower.astype(jnp.bfloat16), v.astype(jnp.bfloat16), preferred_element_type=jnp.float32)
    loff = 16.0 * ldot(c_hi) + ldot(c_lo)
    base = cum + loff[:, 0:1]
    for k in range(TOP_K):
        prob_ref[k:k + 1, :] = exps[k] * inv
        lp_ref[k:k + 1, :] = jnp.sum(jnp.where(hots[k], base, 0.0), axis=0, keepdims=True).astype(jnp.int32)
    cnt_ref[...] = pairs.astype(jnp.int32)


def _router(h1, norm_g, router_w, router_b, *, tm):
    n, d = h1.shape
    e = router_w.shape[-1]
    f32 = jnp.float32
    rwt = router_w.astype(f32).T
    rwh = rwt.astype(jnp.bfloat16)
    rwl = (rwt - rwh.astype(f32)).astype(jnp.bfloat16)
    kn = lambda dt: jax.ShapeDtypeStruct((TOP_K, n), dt)
    kspec = pl.BlockSpec((TOP_K, tm), lambda i: (0, i))
    return pl.pallas_call(
        functools.partial(_router_kernel, n_experts=e),
        out_shape=(jax.ShapeDtypeStruct((n, d), jnp.bfloat16), kn(jnp.int32), kn(f32),
                   jax.ShapeDtypeStruct((n // tm, e, LANES), jnp.int32)),
        grid=(n // tm,),
        in_specs=[pl.BlockSpec((tm, d), lambda i: (i, 0)), _const_spec((1, d)), _const_spec((e, d)),
                  _const_spec((e, d)), _const_spec((e, 1))],
        out_specs=(pl.BlockSpec((tm, d), lambda i: (i, 0)), kspec, kspec,
                   pl.BlockSpec((None, e, LANES), lambda i: (i, 0, 0))),
        compiler_params=pltpu.CompilerParams(dimension_semantics=("arbitrary",), vmem_limit_bytes=VMEM_LIMIT),
        name="router",
    )(h1, norm_g.reshape(1, d).astype(f32), rwh, rwl, router_b.reshape(e, 1).astype(f32))


def _sort_matrix(lp, n_rows, axis):
    if axis == 0:
        shape = (n_rows, lp.shape[1])
        hit = lambda k: lax.broadcasted_iota(jnp.int32, shape, 0) == lp[k:k + 1, :]
    else:
        shape = (lp.shape[0], n_rows)
        hit = lambda k: lax.broadcasted_iota(jnp.int32, shape, 1) == lp[:, k:k + 1]
    m = hit(0)
    for k in range(1, TOP_K):
        m = m | hit(k)
    return m


def _start_runs(n_runs, run_args, src_ref, dst_ref, sem, src_step=RUN_CHUNK):
    def one_run(r, started):
        s, d, ln = run_args(r)
        n_ch = (jnp.maximum(ln, 0) + (RUN_CHUNK - 1)) // RUN_CHUNK

        def one_chunk(q, c):
            pltpu.make_async_copy(src_ref.at[pl.ds(s + q * src_step, RUN_CHUNK)],
                                  dst_ref.at[pl.ds(d + q * RUN_CHUNK, RUN_CHUNK)], sem).start()
            return c

        lax.fori_loop(0, n_ch, one_chunk, 0)
        return started + n_ch

    return lax.fori_loop(0, n_runs, one_run, jnp.int32(0))


def _drain(count, src_ref, dst_ref, sem):
    def wait_chunks(n_chunks):
        def body(q, c):
            pltpu.make_async_copy(src_ref.at[pl.ds(0, n_chunks * RUN_CHUNK)],
                                  dst_ref.at[pl.ds(0, n_chunks * RUN_CHUNK)], sem).wait()
            return c
        return body

    groups = count // WAIT_GROUP
    lax.fori_loop(0, groups, wait_chunks(WAIT_GROUP), 0)
    lax.fori_loop(0, count - groups * WAIT_GROUP, wait_chunks(1), 0)


def _dispatch_kernel(cnt_ref, loff_ref, dst_ref, fill_at_ref, fill_len_ref, xb_ref, lp_ref, xs_ref,
                     xl_ref, pend_ref, sem, *, n_experts):
    i = pl.program_id(0)
    rows = xl_ref.shape[1] - RUN_CHUNK
    d = xl_ref.shape[3]

    @pl.when(i == 0)
    def _():
        for s in range(2):
            xl_ref[s, rows:] = jnp.zeros((RUN_CHUNK, ROW_PACK, d), xl_ref.dtype)
        n_fill = _start_runs(n_experts + 1, lambda r: (rows, fill_at_ref[r], fill_len_ref[r]), xl_ref.at[0], xs_ref,
                             sem, src_step=0)
        _drain(n_fill, xl_ref.at[0], xs_ref, sem)
        pend_ref[0] = 0

    src = xl_ref.at[i % 2]
    perm = _sort_matrix(lp_ref[...], rows * ROW_PACK, 0).astype(jnp.bfloat16)
    sorted_rows = jnp.dot(perm, xb_ref[...], preferred_element_type=jnp.float32).astype(jnp.bfloat16)
    src[:rows] = sorted_rows.reshape(rows, ROW_PACK, d)

    def run_args(e):
        r = i * n_experts + e
        return loff_ref[r], dst_ref[r], cnt_ref[r]

    _drain(pend_ref[0], src, xs_ref, sem)
    pend_ref[0] = _start_runs(n_experts, run_args, src, xs_ref, sem)

    @pl.when(i == pl.num_programs(0) - 1)
    def _():
        _drain(pend_ref[0], src, xs_ref, sem)


def _dispatch(xb, lp, cnt_flat, loff_flat, dst_flat, fill_at, fill_len, m_pad, local_pairs, *, tm, n_experts):
    n, d = xb.shape
    grid_spec = pltpu.PrefetchScalarGridSpec(
        num_scalar_prefetch=5,
        grid=(n // tm,),
        in_specs=[pl.BlockSpec((tm, d), lambda i, *_: (i, 0)), pl.BlockSpec((TOP_K, tm), lambda i, *_: (0, i))],
        out_specs=pl.BlockSpec(memory_space=pl.ANY),
        scratch_shapes=[pltpu.VMEM((2, local_pairs + RUN_CHUNK, ROW_PACK, d), jnp.bfloat16),
                        pltpu.SMEM((1,), jnp.int32), pltpu.SemaphoreType.DMA],
    )
    return pl.pallas_call(
        functools.partial(_dispatch_kernel, n_experts=n_experts),
        out_shape=jax.ShapeDtypeStruct((m_pad, ROW_PACK, d), jnp.bfloat16),
        grid_spec=grid_spec,
        compiler_params=pltpu.CompilerParams(dimension_semantics=("arbitrary",), vmem_limit_bytes=VMEM_LIMIT),
        name="dispatch",
    )(cnt_flat, loff_flat, dst_flat, fill_at, fill_len, xb, lp)


def _expert_kernel(te_ref, nu_ref, tj_ref, ilo_ref, ihi_ref, cnt_ref, loff_ref, pre_ref, bfill_at_ref, bfill_len_ref,
                   xs_ref, wgu_ref, bgu_ref, wd_ref, bd_ref, ys_ref, yb_ref, pend_ref, sem,
                   *, n_experts, blk_rows):
    j = pl.program_id(0)
    tm, _, d = xs_ref.shape

    @pl.when(j == 0)
    def _():
        for s in range(2):
            yb_ref[s, tm:] = jnp.zeros((RUN_CHUNK, ROW_PACK, d), yb_ref.dtype)
        n_blocks = ys_ref.shape[0] // blk_rows
        n_fill = _start_runs(n_blocks, lambda r: (tm, bfill_at_ref[r], bfill_len_ref[r]), yb_ref.at[0],
                             ys_ref, sem, src_step=0)
        _drain(n_fill, yb_ref.at[0], ys_ref, sem)
        pend_ref[0] = 0

    src = yb_ref.at[j % 2]

    @pl.when(j < nu_ref[0])
    def _():
        d_exp = wd_ref.shape[0]
        x = xs_ref[...].reshape(tm * ROW_PACK, d)
        gu = jnp.dot(x, wgu_ref[...], preferred_element_type=jnp.float32) + bgu_ref[...]
        g = jnp.minimum(gu[:, :d_exp], SWIGLU_LIMIT)
        lin = jnp.clip(gu[:, d_exp:], -SWIGLU_LIMIT, SWIGLU_LIMIT)
        act = g * _sigmoid(SWIGLU_ALPHA * g) * (lin + 1.0)
        y = jnp.dot(act.astype(jnp.bfloat16), wd_ref[...], preferred_element_type=jnp.float32) + bd_ref[...]
        src[:tm] = y.astype(jnp.bfloat16).reshape(tm, ROW_PACK, d)

    _drain(pend_ref[0], src, ys_ref, sem)
    pend_ref[0] = 0

    @pl.when(j < nu_ref[0])
    def _():
        e = te_ref[j]
        base = tj_ref[j] * tm
        i0 = ilo_ref[j]

        def run_args(r):
            it = i0 + r
            a = pre_ref[it * n_experts + e] - base
            b = a + cnt_ref[it * n_experts + e]
            ac = jnp.maximum(a, 0)
            return ac, it * blk_rows + loff_ref[it * n_experts + e] + (ac - a), jnp.minimum(b, tm) - ac

        pend_ref[0] = _start_runs(ihi_ref[j] - i0, run_args, src, ys_ref, sem)

    @pl.when(j == pl.num_programs(0) - 1)
    def _():
        _drain(pend_ref[0], src, ys_ref, sem)


def _experts(xs, sched, w_gate_up, b_gate_up, w_down, b_down, n_tok_tiles, blk_rows, *, tm):
    m_pad = xs.shape[0]
    e, d, d_gu = w_gate_up.shape
    d_exp = w_down.shape[1]
    bf16, f32 = jnp.bfloat16, jnp.float32
    row_map = lambda j, te, nu, *_: (jnp.minimum(j, nu[0] - 1), 0, 0)
    w_map = lambda j, te, *_: (te[j], 0, 0)
    grid_spec = pltpu.PrefetchScalarGridSpec(
        num_scalar_prefetch=len(sched),
        grid=(m_pad // tm,),
        in_specs=[pl.BlockSpec((tm, ROW_PACK, d), row_map),
                  pl.BlockSpec((None, d, d_gu), w_map), pl.BlockSpec((None, 1, d_gu), w_map),
                  pl.BlockSpec((None, d_exp, d), w_map), pl.BlockSpec((None, 1, d), w_map)],
        out_specs=pl.BlockSpec(memory_space=pl.ANY),
        scratch_shapes=[pltpu.VMEM((2, tm + RUN_CHUNK, ROW_PACK, d), bf16),
                        pltpu.SMEM((1,), jnp.int32), pltpu.SemaphoreType.DMA],
    )
    return pl.pallas_call(
        functools.partial(_expert_kernel, n_experts=e, blk_rows=blk_rows),
        out_shape=jax.ShapeDtypeStruct((n_tok_tiles * blk_rows, ROW_PACK, d), bf16),
        grid_spec=grid_spec,
        compiler_params=pltpu.CompilerParams(dimension_semantics=("arbitrary",), vmem_limit_bytes=VMEM_LIMIT),
        name="experts",
    )(*sched, xs, w_gate_up.astype(bf16), b_gate_up.reshape(e, 1, d_gu).astype(f32),
      w_down.astype(bf16), b_down.reshape(e, 1, d).astype(f32))


def _combine_kernel(h_ref, ys_ref, lp_ref, lpt_ref, p_ref, g_ref, out_ref):
    tm, d = h_ref.shape
    pairs = ys_ref.shape[0] - RUN_CHUNK
    rows = pairs * ROW_PACK
    y = ys_ref[:pairs].reshape(rows, d).astype(jnp.float32)
    lp = lp_ref[...]
    riota = lax.broadcasted_iota(jnp.int32, (rows, tm), 0)
    wmat = jnp.where(riota == lp[0:1, :], p_ref[0:1, :], 0.0)
    for k in range(1, TOP_K):
        wmat = wmat + jnp.where(riota == lp[k:k + 1, :], p_ref[k:k + 1, :], 0.0)
    pw = jnp.sum(wmat, axis=1, keepdims=True)
    unsort = _sort_matrix(lpt_ref[...], rows, 1).astype(jnp.bfloat16)
    acc = h_ref[...] + jnp.dot(unsort, (y * pw).astype(jnp.bfloat16), preferred_element_type=jnp.float32)
    out_ref[...] = acc * lax.rsqrt(jnp.mean(acc * acc, axis=-1, keepdims=True) + RMS_EPS) * g_ref[...]


def _combine(h1, ys, lp, lpt, prob, norm_g, blk_rows, *, tm):
    n, d = h1.shape
    f32 = jnp.float32
    kspec = pl.BlockSpec((TOP_K, tm), lambda i: (0, i))
    return pl.pallas_call(
        _combine_kernel,
        out_shape=jax.ShapeDtypeStruct((n, d), f32),
        grid=(n // tm,),
        in_specs=[pl.BlockSpec((tm, d), lambda i: (i, 0)),
                  pl.BlockSpec((blk_rows, ROW_PACK, d), lambda i: (i, 0, 0)),
                  kspec, pl.BlockSpec((tm, TOP_K), lambda i: (i, 0)), kspec, _const_spec((1, d))],
        out_specs=pl.BlockSpec((tm, d), lambda i: (i, 0)),
        compiler_params=pltpu.CompilerParams(dimension_semantics=("arbitrary",), vmem_limit_bytes=VMEM_LIMIT),
        name="combine",
    )(h1, ys, lp, lpt, prob, norm_g.reshape(1, d).astype(f32))


def _tile_rows(n, target):
    t = min(target, n)
    while n % t:
        t //= 2
    return t


def _moe_schedule(cnt, tm_tok, tm_e, blk_rows):
    n_t, e = cnt.shape
    i32 = jnp.int32
    total = jnp.sum(cnt, axis=0)
    tiles_per_e = (total + RUN_CHUNK + tm_e - 1) // tm_e
    tile_end = jnp.cumsum(tiles_per_e)
    tile_begin = tile_end - tiles_per_e
    n_used = tile_end[-1:]
    n_tiles = ((TOP_K * tm_tok + e) * n_t) // (ROW_PACK * tm_e) + 2 * e + 1
    tile_id = jnp.minimum(jnp.arange(n_tiles, dtype=i32), n_used[0] - 1)
    tile_expert = jnp.sum(tile_id[:, None] >= tile_end[None, :], axis=1).astype(i32)
    tile_j = tile_id - tile_begin[tile_expert]
    pre = jnp.cumsum(cnt, axis=0) - cnt
    loff = jnp.cumsum(cnt, axis=1) - cnt
    dst = tile_begin[None, :] * tm_e + pre
    pre_t = pre[:, tile_expert].T
    end_t = pre_t + cnt[:, tile_expert].T
    ilo = jnp.sum(end_t <= (tile_j * tm_e)[:, None], axis=1).astype(i32)
    ihi = jnp.sum(pre_t < ((tile_j + 1) * tm_e)[:, None], axis=1).astype(i32)
    gap = tiles_per_e * tm_e - total
    fill_len = (gap + RUN_CHUNK - 1) // RUN_CHUNK * RUN_CHUNK
    fill_at = jnp.concatenate([tile_end * tm_e - fill_len, n_used * tm_e])
    fill_len = jnp.concatenate([fill_len, (n_tiles - n_used) * tm_e])
    bfill_len = (blk_rows - jnp.sum(cnt, axis=1) + RUN_CHUNK - 1) // RUN_CHUNK * RUN_CHUNK
    bfill_at = (jnp.arange(n_t, dtype=i32) + 1) * blk_rows - bfill_len
    flat = lambda a: a.reshape(-1).astype(i32)
    return dict(n_tiles=n_tiles, tile_expert=tile_expert, n_used=n_used.astype(i32), tile_j=tile_j.astype(i32),
                ilo=ilo, ihi=ihi, cnt=flat(cnt), loff=flat(loff), pre=flat(pre), dst=flat(dst),
                fill_at=flat(fill_at), fill_len=flat(fill_len), bfill_at=flat(bfill_at), bfill_len=flat(bfill_len))


def _moe_and_final_norm(h1, norm_ffn_g, router_w, router_b, w_gate_up, b_gate_up, w_down, b_down, norm_final_g):
    n, d = h1.shape
    e = router_w.shape[-1]
    tm_tok = _tile_rows(n, 256)
    tm_e = _tile_rows(n, 512) // ROW_PACK
    local_rows = -(-(TOP_K * tm_tok + e) // (ROW_PACK * RUN_CHUNK)) * (ROW_PACK * RUN_CHUNK)
    blk_rows = local_rows // ROW_PACK + RUN_CHUNK
    xb, lp, prob, cnt3 = _router(h1, norm_ffn_g, router_w, router_b, tm=tm_tok)
    s = _moe_schedule(cnt3[:, :, 0], tm_tok, tm_e, blk_rows)
    xs = _dispatch(xb, lp, s["cnt"], s["loff"], s["dst"], s["fill_at"], s["fill_len"], s["n_tiles"] * tm_e,
                   local_rows // ROW_PACK, tm=tm_tok, n_experts=e)
    sched = (s["tile_expert"], s["n_used"], s["tile_j"], s["ilo"], s["ihi"], s["cnt"], s["loff"], s["pre"],
             s["bfill_at"], s["bfill_len"])
    ys = _experts(xs, sched, w_gate_up, b_gate_up, w_down, b_down, n // tm_tok, blk_rows, tm=tm_e)
    return _combine(h1, ys, lp, lp.T, prob, norm_final_g, blk_rows, tm=tm_tok)


def kernel(x, meta_tokens, norm_mix_g, w_in, ssm_a_re, ssm_a_im, ssm_log_dt, ssm_b_re, ssm_b_im, ssm_c_re, ssm_c_im, ssm_d, w_s5_lin, w_s5_gate, conv_dw_w, conv_dw_b, conv_ln_g, conv_ln_b, w_conv_out, w_out, norm_ffn_g, router_w, router_b, w_gate_up, b_gate_up, w_down, b_down, norm_final_g):
    assert norm_mix_g.shape[0] == 1, "single-layer block"
    bsz, seq, d_model = x.shape
    h1 = _mixer(x, meta_tokens, norm_mix_g[0], w_in[0], ssm_a_re[0], ssm_a_im[0], ssm_log_dt[0], ssm_b_re[0],
                ssm_b_im[0], ssm_c_re[0], ssm_c_im[0], ssm_d[0], w_s5_lin[0], w_s5_gate[0], conv_dw_w[0],
                conv_dw_b[0], conv_ln_g[0], conv_ln_b[0], w_conv_out[0], w_out[0], t_chunk=32)
    out = _moe_and_final_norm(h1.reshape(bsz * seq, d_model), norm_ffn_g[0], router_w[0], router_b[0],
                              w_gate_up[0], b_gate_up[0], w_down[0], b_down[0], norm_final_g)
    return out.reshape(bsz, seq, d_model).astype(x.dtype)
```

```python
import functools
import math

import jax
import jax.numpy as jnp
from jax import lax
from jax.experimental import pallas as pl
from jax.experimental.pallas import tpu as pltpu

RMS_EPS = 1e-6
LN_EPS = 1e-5
TOP_K = 4
SWIGLU_LIMIT = 7.0
SWIGLU_ALPHA = 1.702

LANES = 128
SUBLANES = 8
VMEM_LIMIT = 56 * 1024 * 1024
SCAN_CHAINS = 8
CONV_OUT_TILE = 8


def _sigmoid(v):
    return 1.0 / (1.0 + jnp.exp(-v))


def _const_spec(shape):
    nd = len(shape)
    return pl.BlockSpec(shape, lambda *_: (0,) * nd)


def _mixer_kernel(x_ref, meta_ref, g_ref, win_ref, ar_ref, ai_ref, bd_ref, ct_ref, dskip_ref,
                  wlin_ref, wgate_ref, dww_ref, dwb_ref, lng_ref, lnb_ref, wco_ref, wout_ref,
                  out_ref,
                  xs_ref, st_ref, state_ref, cbuf_ref, conv_ref,
                  *, t_chunk, d_ssm, d_conv, conv_w, sb_lanes):
    i = pl.program_id(0)
    bsz = SUBLANES
    rows = t_chunk * bsz
    d_model = xs_ref.shape[1]
    n_sb = d_ssm // LANES
    hist = (conv_w - 1) * bsz

    @pl.when(i == 0)
    def _():
        state_ref[...] = jnp.zeros_like(state_ref)
        cbuf_ref[...] = jnp.zeros_like(cbuf_ref)
        for t in range(t_chunk):
            xs_ref[t * bsz:(t + 1) * bsz, :] = jnp.broadcast_to(meta_ref[t:t + 1, :], (bsz, d_model))

    @pl.when(i > 0)
    def _():
        for t in range(t_chunk):
            xs_ref[t * bsz:(t + 1) * bsz, :] = x_ref[:, pl.ds(t, 1), :].reshape(bsz, d_model)

    xs = xs_ref[...]
    hn = xs * lax.rsqrt(jnp.mean(xs * xs, axis=-1, keepdims=True) + RMS_EPS) * g_ref[...]
    hn_bf = hn.astype(jnp.bfloat16)
    s1 = d_ssm
    s2 = s1 + 2 * d_conv
    in_proj = lambda lo, hi: jnp.dot(hn_bf, win_ref[:, lo:hi], preferred_element_type=jnp.float32)

    zc = in_proj(s1, s2)
    cbuf_ref[hist:hist + rows, :] = zc[:, :d_conv] * _sigmoid(zc[:, d_conv:])

    u = in_proj(0, s1)
    u_bf = u.astype(jnp.bfloat16)
    for sb in range(n_sb):
        st_ref[:, sb * 2 * sb_lanes:(sb + 1) * 2 * sb_lanes] = jnp.dot(
            u_bf[:, sb * LANES:(sb + 1) * LANES], bd_ref[sb], preferred_element_type=jnp.float32)

    for lb in range(d_conv // LANES):
        ls = slice(lb * LANES, (lb + 1) * LANES)
        wk = [jnp.broadcast_to(dww_ref[k:k + 1, ls], (bsz, LANES)) for k in range(conv_w)]
        for t0 in range(0, t_chunk, CONV_OUT_TILE):
            acc = [None] * CONV_OUT_TILE
            for tt in range(t0, t0 + CONV_OUT_TILE + conv_w - 1):
                v = cbuf_ref[tt * bsz:(tt + 1) * bsz, ls]
                for o in range(CONV_OUT_TILE):
                    k = tt - (t0 + o)
                    if 0 <= k < conv_w:
                        acc[o] = wk[k] * v if acc[o] is None else acc[o] + wk[k] * v
            for o in range(CONV_OUT_TILE):
                conv_ref[(t0 + o) * bsz:(t0 + o + 1) * bsz, ls] = acc[o]
    cbuf_ref[0:hist, :] = cbuf_ref[rows:rows + hist, :]

    zg = in_proj(s2, win_ref.shape[1])

    n_lb = sb_lanes // LANES
    blocks = []
    for sb in range(n_sb):
        base = sb * 2 * sb_lanes
        for j in range(n_lb):
            blocks.append((slice(base + j * LANES, base + (j + 1) * LANES),
                           slice(base + sb_lanes + j * LANES, base + sb_lanes + (j + 1) * LANES),
                           slice(sb * sb_lanes + j * LANES, sb * sb_lanes + (j + 1) * LANES)))
    for g0 in range(0, len(blocks), SCAN_CHAINS):
        grp = blocks[g0:g0 + SCAN_CHAINS]
        ar = [jnp.broadcast_to(ar_ref[:, la], (bsz, LANES)) for _, _, la in grp]
        ai = [jnp.broadcast_to(ai_ref[:, la], (bsz, LANES)) for _, _, la in grp]
        sre = [state_ref[:, lre] for lre, _, _ in grp]
        sim = [state_ref[:, lim] for _, lim, _ in grp]
        for t in range(t_chunk):
            r = slice(t * bsz, (t + 1) * bsz)
            for c, (lre, lim, _) in enumerate(grp):
                nre = ar[c] * sre[c] - ai[c] * sim[c] + st_ref[r, lre]
                nim = ar[c] * sim[c] + ai[c] * sre[c] + st_ref[r, lim]
                st_ref[r, lre] = nre
                st_ref[r, lim] = nim
                sre[c], sim[c] = nre, nim
        for c, (lre, lim, _) in enumerate(grp):
            state_ref[:, lre] = sre[c]
            state_ref[:, lim] = sim[c]
    ys = []
    for sb in range(n_sb):
        ys.append(jnp.dot(st_ref[:, sb * 2 * sb_lanes:(sb + 1) * 2 * sb_lanes].astype(jnp.bfloat16),
                          ct_ref[sb], preferred_element_type=jnp.float32))
    y = jnp.concatenate(ys, axis=-1) if n_sb > 1 else ys[0]
    y = y + dskip_ref[...] * u
    ya = 0.5 * y * (1.0 + lax.erf(y * (1.0 / math.sqrt(2.0))))
    ya_bf = ya.astype(jnp.bfloat16)
    ya2 = (jnp.dot(ya_bf, wlin_ref[...], preferred_element_type=jnp.float32)
           * _sigmoid(jnp.dot(ya_bf, wgate_ref[...], preferred_element_type=jnp.float32)))

    c = conv_ref[...] + dwb_ref[...]
    mu = jnp.mean(c, axis=-1, keepdims=True)
    cc = c - mu
    var = jnp.mean(cc * cc, axis=-1, keepdims=True)
    c = cc * lax.rsqrt(var + LN_EPS) * lng_ref[...] + lnb_ref[...]
    c = c * _sigmoid(c)
    yb = jnp.dot(c.astype(jnp.bfloat16), wco_ref[...], preferred_element_type=jnp.float32)

    merged = _sigmoid(zg[:, :d_model]) * ya2 + _sigmoid(zg[:, d_model:]) * yb
    h1 = xs + jnp.dot(merged.astype(jnp.bfloat16), wout_ref[...], preferred_element_type=jnp.float32)
    xs_ref[...] = h1
    for t in range(t_chunk):
        out_ref[:, pl.ds(t, 1), :] = xs_ref[t * bsz:(t + 1) * bsz, :].reshape(bsz, 1, d_model)


def _s5_discretize(a_re, a_im, log_dt, b_re, b_im, c_re, c_im):
    f32 = jnp.float32
    g, p = a_re.shape
    h = b_re.shape[-1]
    gps = LANES // h
    n_sb = g // gps
    dt = jnp.exp(log_dt.astype(f32))[:, None]
    decay = jnp.exp(a_re * dt)
    abar_re = decay * jnp.cos(a_im * dt)
    abar_im = decay * jnp.sin(a_im * dt)
    den = a_re * a_re + a_im * a_im
    nr = abar_re - 1.0
    coef_re = (nr * a_re + abar_im * a_im) / den
    coef_im = (abar_im * a_re - nr * a_im) / den
    bd_re = coef_re[..., None] * b_re - coef_im[..., None] * b_im
    bd_im = coef_re[..., None] * b_im + coef_im[..., None] * b_re
    eye = jnp.eye(gps, dtype=f32)

    def blockdiag_in(w):
        w = w.reshape(n_sb, gps, p, h)
        return jnp.einsum('sgph,gk->sghkp', w, eye).reshape(n_sb, gps * h, gps * p)

    def blockdiag_out(w):
        w = w.reshape(n_sb, gps, h, p)
        return jnp.einsum('sghp,gk->skpgh', w, eye).reshape(n_sb, gps * p, gps * h)

    bd = jnp.concatenate([blockdiag_in(bd_re), blockdiag_in(bd_im)], axis=-1)
    ct = jnp.concatenate([blockdiag_out(c_re), -blockdiag_out(c_im)], axis=1)
    return (abar_re.reshape(1, g * p), abar_im.reshape(1, g * p),
            bd.astype(jnp.bfloat16), ct.astype(jnp.bfloat16), gps * p)


def _mixer(x, meta_tokens, norm_g, w_in, a_re, a_im, log_dt, b_re, b_im, c_re, c_im, d_skip,
           w_s5_lin, w_s5_gate, dw_w, dw_b, ln_g, ln_b, w_conv_out, w_out, *, t_chunk):
    bsz, seq, d_model = x.shape
    n_meta = meta_tokens.shape[0]
    g, p = a_re.shape
    d_ssm = g * b_re.shape[-1]
    conv_w, _, d_conv = dw_w.shape
    assert bsz == SUBLANES and seq % t_chunk == 0 and t_chunk >= n_meta and t_chunk % SUBLANES == 0
    assert d_ssm % LANES == 0 and d_conv % LANES == 0
    f32, bf16 = jnp.float32, jnp.bfloat16
    ar, ai, bd, ct, sb_lanes = _s5_discretize(a_re.astype(f32), a_im.astype(f32), log_dt, b_re.astype(f32),
                                              b_im.astype(f32), c_re.astype(f32), c_im.astype(f32))
    meta_chunk = jnp.concatenate([jnp.zeros((t_chunk - n_meta, d_model), f32), meta_tokens.astype(f32)], axis=0)
    rows = t_chunk * bsz
    n_state = g * p
    consts = [
        meta_chunk, norm_g.reshape(1, d_model).astype(f32), w_in.astype(bf16), ar, ai, bd, ct,
        d_skip.reshape(1, d_ssm).astype(f32), w_s5_lin.astype(bf16), w_s5_gate.astype(bf16),
        dw_w.reshape(conv_w, d_conv).astype(f32), dw_b.reshape(1, d_conv).astype(f32),
        ln_g.reshape(1, d_conv).astype(f32), ln_b.reshape(1, d_conv).astype(f32),
        w_conv_out.astype(bf16), w_out.astype(bf16),
    ]
    x_spec = pl.BlockSpec((bsz, t_chunk, d_model), lambda i: (0, jnp.maximum(i - 1, 0), 0))
    kern = functools.partial(_mixer_kernel, t_chunk=t_chunk, d_ssm=d_ssm, d_conv=d_conv, conv_w=conv_w,
                             sb_lanes=sb_lanes)
    return pl.pallas_call(
        kern,
        out_shape=jax.ShapeDtypeStruct((bsz, seq, d_model), f32),
        grid=(seq // t_chunk + 1,),
        in_specs=[x_spec] + [_const_spec(c.shape) for c in consts],
        out_specs=x_spec,
        scratch_shapes=[
            pltpu.VMEM((rows, d_model), f32),
            pltpu.VMEM((rows, 2 * n_state), f32),
            pltpu.VMEM((bsz, 2 * n_state), f32),
            pltpu.VMEM(((conv_w - 1) * bsz + rows, d_conv), f32),
            pltpu.VMEM((rows, d_conv), f32),
        ],
        compiler_params=pltpu.CompilerParams(dimension_semantics=("arbitrary",), vmem_limit_bytes=VMEM_LIMIT),
        name="mixer",
    )(x.astype(f32), *consts)


ROW_PACK = 2
RUN_CHUNK = 32
WAIT_GROUP = 8


def _router_kernel(h_ref, g_ref, rwh_ref, rwl_ref, rb_ref, xb_ref, lp_ref, prob_ref, cnt_ref, *, n_experts):
    tm = h_ref.shape[0]
    h = h_ref[...]
    hn = h * lax.rsqrt(jnp.mean(h * h, axis=-1, keepdims=True) + RMS_EPS) * g_ref[...]
    hn_hi = hn.astype(jnp.bfloat16)
    xb_ref[...] = hn_hi
    hn_lo = (hn - hn_hi.astype(jnp.float32)).astype(jnp.bfloat16)
    nt = (((1,), (1,)), ((), ()))
    dot = lambda a, b: lax.dot_general(a, b, nt, preferred_element_type=jnp.float32)
    lg = dot(rwh_ref[...], hn_hi) + dot(rwh_ref[...], hn_lo) + dot(rwl_ref[...], hn_hi) + rb_ref[...]
    eidx = lax.broadcasted_iota(jnp.int32, (n_experts, tm), 0)
    work = lg
    vals, hots = [], []
    for _ in range(TOP_K):
        m = jnp.max(work, axis=0, keepdims=True)
        sel = jnp.min(jnp.where(work == m, eidx, n_experts), axis=0, keepdims=True)
        hot = eidx == sel
        vals.append(m)
        hots.append(hot)
        work = jnp.where(hot, -jnp.inf, work)
    exps = [jnp.exp(v - vals[0]) for v in vals]
    tot = exps[0]
    for e in exps[1:]:
        tot = tot + e
    inv = 1.0 / tot
    oh = hots[0].astype(jnp.float32)
    for hot in hots[1:]:
        oh = oh + hot.astype(jnp.float32)
    upper = (lax.broadcasted_iota(jnp.int32, (tm, tm), 0) < lax.broadcasted_iota(jnp.int32, (tm, tm), 1))
    cum = jnp.dot(oh.astype(jnp.bfloat16), upper.astype(jnp.bfloat16), preferred_element_type=jnp.float32)
    cnt = jnp.sum(oh, axis=1, keepdims=True)
    lower = (lax.broadcasted_iota(jnp.int32, (n_experts, n_experts), 1)
             < lax.broadcasted_iota(jnp.int32, (n_experts, n_experts), 0))
    cnt_b = jnp.broadcast_to(cnt, (n_experts, LANES))
    pairs = jnp.floor((cnt_b + 1.0) * 0.5)
    cnt_b = 2.0 * pairs
    c_hi = jnp.floor(cnt_b * (1.0 / 16.0))
    c_lo = cnt_b - 16.0 * c_hi
    ldot = lambda v: jnp.dot(lower.astype(jnp.bfloat16), v.astype(jnp.bfloat16), preferred_element_type=jnp.float32)
    loff = 16.0 * ldot(c_hi) + ldot(c_lo)
    base = cum + loff[:, 0:1]
    for k in range(TOP_K):
        prob_ref[k:k + 1, :] = exps[k] * inv
        lp_ref[k:k + 1, :] = jnp.sum(jnp.where(hots[k], base, 0.0), axis=0, keepdims=True).astype(jnp.int32)
    cnt_ref[...] = pairs.astype(jnp.int32)


def _router(h1, norm_g, router_w, router_b, *, tm):
    n, d = h1.shape
    e = router_w.shape[-1]
    f32 = jnp.float32
    rwt = router_w.astype(f32).T
    rwh = rwt.astype(jnp.bfloat16)
    rwl = (rwt - rwh.astype(f32)).astype(jnp.bfloat16)
    kn = lambda dt: jax.ShapeDtypeStruct((TOP_K, n), dt)
    kspec = pl.BlockSpec((TOP_K, tm), lambda i: (0, i))
    return pl.pallas_call(
        functools.partial(_router_kernel, n_experts=e),
        out_shape=(jax.ShapeDtypeStruct((n, d), jnp.bfloat16), kn(jnp.int32), kn(f32),
                   jax.ShapeDtypeStruct((n // tm, e, LANES), jnp.int32)),
        grid=(n // tm,),
        in_specs=[pl.BlockSpec((tm, d), lambda i: (i, 0)), _const_spec((1, d)), _const_spec((e, d)),
                  _const_spec((e, d)), _const_spec((e, 1))],
        out_specs=(pl.BlockSpec((tm, d), lambda i: (i, 0)), kspec, kspec,
                   pl.BlockSpec((None, e, LANES), lambda i: (i, 0, 0))),
        compiler_params=pltpu.CompilerParams(dimension_semantics=("arbitrary",), vmem_limit_bytes=VMEM_LIMIT),
        name="router",
    )(h1, norm_g.reshape(1, d).astype(f32), rwh, rwl, router_b.reshape(e, 1).astype(f32))


def _sort_matrix(lp, n_rows, axis):
    if axis == 0:
        shape = (n_rows, lp.shape[1])
        hit = lambda k: lax.broadcasted_iota(jnp.int32, shape, 0) == lp[k:k + 1, :]
    else:
        shape = (lp.shape[0], n_rows)
        hit = lambda k: lax.broadcasted_iota(jnp.int32, shape, 1) == lp[:, k:k + 1]
    m = hit(0)
    for k in range(1, TOP_K):
        m = m | hit(k)
    return m


def _start_runs(n_runs, run_args, src_ref, dst_ref, sem, src_step=RUN_CHUNK):
    def one_run(r, started):
        s, d, ln = run_args(r)
        n_ch = (jnp.maximum(ln, 0) + (RUN_CHUNK - 1)) // RUN_CHUNK

        def one_chunk(q, c):
            pltpu.make_async_copy(src_ref.at[pl.ds(s + q * src_step, RUN_CHUNK)],
                                  dst_ref.at[pl.ds(d + q * RUN_CHUNK, RUN_CHUNK)], sem).start()
            return c

        lax.fori_loop(0, n_ch, one_chunk, 0)
        return started + n_ch

    return lax.fori_loop(0, n_runs, one_run, jnp.int32(0))


def _drain(count, src_ref, dst_ref, sem):
    def wait_chunks(n_chunks):
        def body(q, c):
            pltpu.make_async_copy(src_ref.at[pl.ds(0, n_chunks * RUN_CHUNK)],
                                  dst_ref.at[pl.ds(0, n_chunks * RUN_CHUNK)], sem).wait()
            return c
        return body

    groups = count // WAIT_GROUP
    lax.fori_loop(0, groups, wait_chunks(WAIT_GROUP), 0)
    lax.fori_loop(0, count - groups * WAIT_GROUP, wait_chunks(1), 0)


def _dispatch_kernel(cnt_ref, loff_ref, dst_ref, fill_at_ref, fill_len_ref, xb_ref, lp_ref, xs_ref,
                     xl_ref, pend_ref, sem, *, n_experts):
    i = pl.program_id(0)
    rows = xl_ref.shape[1] - RUN_CHUNK
    d = xl_ref.shape[3]

    @pl.when(i == 0)
    def _():
        for s in range(2):
            xl_ref[s, rows:] = jnp.zeros((RUN_CHUNK, ROW_PACK, d), xl_ref.dtype)
        n_fill = _start_runs(n_experts + 1, lambda r: (rows, fill_at_ref[r], fill_len_ref[r]), xl_ref.at[0], xs_ref,
                             sem, src_step=0)
        _drain(n_fill, xl_ref.at[0], xs_ref, sem)
        pend_ref[0] = 0

    src = xl_ref.at[i % 2]
    perm = _sort_matrix(lp_ref[...], rows * ROW_PACK, 0).astype(jnp.bfloat16)
    sorted_rows = jnp.dot(perm, xb_ref[...], preferred_element_type=jnp.float32).astype(jnp.bfloat16)
    src[:rows] = sorted_rows.reshape(rows, ROW_PACK, d)

    def run_args(e):
        r = i * n_experts + e
        return loff_ref[r], dst_ref[r], cnt_ref[r]

    _drain(pend_ref[0], src, xs_ref, sem)
    pend_ref[0] = _start_runs(n_experts, run_args, src, xs_ref, sem)

    @pl.when(i == pl.num_programs(0) - 1)
    def _():
        _drain(pend_ref[0], src, xs_ref, sem)


def _dispatch(xb, lp, cnt_flat, loff_flat, dst_flat, fill_at, fill_len, m_pad, local_pairs, *, tm, n_experts):
    n, d = xb.shape
    grid_spec = pltpu.PrefetchScalarGridSpec(
        num_scalar_prefetch=5,
        grid=(n // tm,),
        in_specs=[pl.BlockSpec((tm, d), lambda i, *_: (i, 0)), pl.BlockSpec((TOP_K, tm), lambda i, *_: (0, i))],
        out_specs=pl.BlockSpec(memory_space=pl.ANY),
        scratch_shapes=[pltpu.VMEM((2, local_pairs + RUN_CHUNK, ROW_PACK, d), jnp.bfloat16),
                        pltpu.SMEM((1,), jnp.int32), pltpu.SemaphoreType.DMA],
    )
    return pl.pallas_call(
        functools.partial(_dispatch_kernel, n_experts=n_experts),
        out_shape=jax.ShapeDtypeStruct((m_pad, ROW_PACK, d), jnp.bfloat16),
        grid_spec=grid_spec,
        compiler_params=pltpu.CompilerParams(dimension_semantics=("arbitrary",), vmem_limit_bytes=VMEM_LIMIT),
        name="dispatch",
    )(cnt_flat, loff_flat, dst_flat, fill_at, fill_len, xb, lp)


def _expert_kernel(te_ref, nu_ref, tj_ref, ilo_ref, ihi_ref, cnt_ref, loff_ref, pre_ref, bfill_at_ref, bfill_len_ref,
                   xs_ref, wgu_ref, bgu_ref, wd_ref, bd_ref, ys_ref, wgu_bf_ref, wd_bf_ref, yb_ref, pend_ref, sem,
                   *, n_experts, blk_rows):
    j = pl.program_id(0)
    tm, _, d = xs_ref.shape

    @pl.when(j == 0)
    def _():
        for s in range(2):
            yb_ref[s, tm:] = jnp.zeros((RUN_CHUNK, ROW_PACK, d), yb_ref.dtype)
        n_blocks = ys_ref.shape[0] // blk_rows
        n_fill = _start_runs(n_blocks, lambda r: (tm, bfill_at_ref[r], bfill_len_ref[r]), yb_ref.at[0],
                             ys_ref, sem, src_step=0)
        _drain(n_fill, yb_ref.at[0], ys_ref, sem)
        pend_ref[0] = 0

    src = yb_ref.at[j % 2]

    @pl.when(j < nu_ref[0])
    def _():
        d_exp = wd_ref.shape[0]

        @pl.when(tj_ref[j] == 0)
        def _():
            wgu_bf_ref[...] = wgu_ref[...].astype(jnp.bfloat16)
            wd_bf_ref[...] = wd_ref[...].astype(jnp.bfloat16)

        x = xs_ref[...].reshape(tm * ROW_PACK, d)
        gu = jnp.dot(x, wgu_bf_ref[...], preferred_element_type=jnp.float32) + bgu_ref[...]
        g = jnp.minimum(gu[:, :d_exp], SWIGLU_LIMIT)
        lin = jnp.clip(gu[:, d_exp:], -SWIGLU_LIMIT, SWIGLU_LIMIT)
        act = g * _sigmoid(SWIGLU_ALPHA * g) * (lin + 1.0)
        y = jnp.dot(act.astype(jnp.bfloat16), wd_bf_ref[...], preferred_element_type=jnp.float32) + bd_ref[...]
        src[:tm] = y.astype(jnp.bfloat16).reshape(tm, ROW_PACK, d)

    _drain(pend_ref[0], src, ys_ref, sem)
    pend_ref[0] = 0

    @pl.when(j < nu_ref[0])
    def _():
        e = te_ref[j]
        base = tj_ref[j] * tm
        i0 = ilo_ref[j]

        def run_args(r):
            it = i0 + r
            a = pre_ref[it * n_experts + e] - base
            b = a + cnt_ref[it * n_experts + e]
            ac = jnp.maximum(a, 0)
            return ac, it * blk_rows + loff_ref[it * n_experts + e] + (ac - a), jnp.minimum(b, tm) - ac

        pend_ref[0] = _start_runs(ihi_ref[j] - i0, run_args, src, ys_ref, sem)

    @pl.when(j == pl.num_programs(0) - 1)
    def _():
        _drain(pend_ref[0], src, ys_ref, sem)


def _experts(xs, sched, w_gate_up, b_gate_up, w_down, b_down, n_tok_tiles, blk_rows, *, tm):
    m_pad = xs.shape[0]
    e, d, d_gu = w_gate_up.shape
    d_exp = w_down.shape[1]
    bf16, f32 = jnp.bfloat16, jnp.float32
    row_map = lambda j, te, nu, *_: (jnp.minimum(j, nu[0] - 1), 0, 0)
    w_map = lambda j, te, *_: (te[j], 0, 0)
    grid_spec = pltpu.PrefetchScalarGridSpec(
        num_scalar_prefetch=len(sched),
        grid=(m_pad // tm,),
        in_specs=[pl.BlockSpec((tm, ROW_PACK, d), row_map),
                  pl.BlockSpec((None, d, d_gu), w_map), pl.BlockSpec((None, 1, d_gu), w_map),
                  pl.BlockSpec((None, d_exp, d), w_map), pl.BlockSpec((None, 1, d), w_map)],
        out_specs=pl.BlockSpec(memory_space=pl.ANY),
        scratch_shapes=[pltpu.VMEM((d, d_gu), bf16), pltpu.VMEM((d_exp, d), bf16),
                        pltpu.VMEM((2, tm + RUN_CHUNK, ROW_PACK, d), bf16),
                        pltpu.SMEM((1,), jnp.int32), pltpu.SemaphoreType.DMA],
    )
    return pl.pallas_call(
        functools.partial(_expert_kernel, n_experts=e, blk_rows=blk_rows),
        out_shape=jax.ShapeDtypeStruct((n_tok_tiles * blk_rows, ROW_PACK, d), bf16),
        grid_spec=grid_spec,
        compiler_params=pltpu.CompilerParams(dimension_semantics=("arbitrary",), vmem_limit_bytes=VMEM_LIMIT),
        name="experts",
    )(*sched, xs, w_gate_up.astype(f32), b_gate_up.reshape(e, 1, d_gu).astype(f32),
      w_down.astype(f32), b_down.reshape(e, 1, d).astype(f32))


def _combine_kernel(h_ref, ys_ref, lpt_ref, pt_ref, g_ref, out_ref):
    tm, d = h_ref.shape
    pairs = ys_ref.shape[0] - RUN_CHUNK
    rows = pairs * ROW_PACK
    y = ys_ref[:pairs].reshape(rows, d)
    col = lax.broadcasted_iota(jnp.int32, (tm, rows), 1)
    w = jnp.zeros((tm, rows), jnp.float32)
    for k in range(TOP_K):
        w = jnp.where(col == lpt_ref[:, k:k + 1], pt_ref[:, k:k + 1], w)
    w_hi = w.astype(jnp.bfloat16)
    w_lo = (w - w_hi.astype(jnp.float32)).astype(jnp.bfloat16)
    acc = (h_ref[...] + jnp.dot(w_hi, y, preferred_element_type=jnp.float32)
           + jnp.dot(w_lo, y, preferred_element_type=jnp.float32))
    out_ref[...] = acc * lax.rsqrt(jnp.mean(acc * acc, axis=-1, keepdims=True) + RMS_EPS) * g_ref[...]


def _combine(h1, ys, lpt, prob_t, norm_g, blk_rows, *, tm):
    n, d = h1.shape
    f32 = jnp.float32
    tspec = pl.BlockSpec((tm, TOP_K), lambda i: (i, 0))
    return pl.pallas_call(
        _combine_kernel,
        out_shape=jax.ShapeDtypeStruct((n, d), f32),
        grid=(n // tm,),
        in_specs=[pl.BlockSpec((tm, d), lambda i: (i, 0)),
                  pl.BlockSpec((blk_rows, ROW_PACK, d), lambda i: (i, 0, 0)),
                  tspec, tspec, _const_spec((1, d))],
        out_specs=pl.BlockSpec((tm, d), lambda i: (i, 0)),
        compiler_params=pltpu.CompilerParams(dimension_semantics=("arbitrary",), vmem_limit_bytes=VMEM_LIMIT),
        name="combine",
    )(h1, ys, lpt, prob_t, norm_g.reshape(1, d).astype(f32))


def _tile_rows(n, target):
    t = min(target, n)
    while n % t:
        t //= 2
    return t


def _moe_schedule(cnt, tm_tok, tm_e, blk_rows):
    n_t, e = cnt.shape
    i32 = jnp.int32
    total = jnp.sum(cnt, axis=0)
    tiles_per_e = (total + RUN_CHUNK + tm_e - 1) // tm_e
    tile_end = jnp.cumsum(tiles_per_e)
    tile_begin = tile_end - tiles_per_e
    n_used = tile_end[-1:]
    n_tiles = ((TOP_K * tm_tok + e) * n_t) // (ROW_PACK * tm_e) + 2 * e + 1
    tile_id = jnp.minimum(jnp.arange(n_tiles, dtype=i32), n_used[0] - 1)
    tile_expert = jnp.sum(tile_id[:, None] >= tile_end[None, :], axis=1).astype(i32)
    tile_j = tile_id - tile_begin[tile_expert]
    pre = jnp.cumsum(cnt, axis=0) - cnt
    loff = jnp.cumsum(cnt, axis=1) - cnt
    dst = tile_begin[None, :] * tm_e + pre
    pre_t = pre[:, tile_expert].T
    end_t = pre_t + cnt[:, tile_expert].T
    ilo = jnp.sum(end_t <= (tile_j * tm_e)[:, None], axis=1).astype(i32)
    ihi = jnp.sum(pre_t < ((tile_j + 1) * tm_e)[:, None], axis=1).astype(i32)
    gap = tiles_per_e * tm_e - total
    fill_len = (gap + RUN_CHUNK - 1) // RUN_CHUNK * RUN_CHUNK
    fill_at = jnp.concatenate([tile_end * tm_e - fill_len, n_used * tm_e])
    fill_len = jnp.concatenate([fill_len, (n_tiles - n_used) * tm_e])
    bfill_len = (blk_rows - jnp.sum(cnt, axis=1) + RUN_CHUNK - 1) // RUN_CHUNK * RUN_CHUNK
    bfill_at = (jnp.arange(n_t, dtype=i32) + 1) * blk_rows - bfill_len
    flat = lambda a: a.reshape(-1).astype(i32)
    return dict(n_tiles=n_tiles, tile_expert=tile_expert, n_used=n_used.astype(i32), tile_j=tile_j.astype(i32),
                ilo=ilo, ihi=ihi, cnt=flat(cnt), loff=flat(loff), pre=flat(pre), dst=flat(dst),
                fill_at=flat(fill_at), fill_len=flat(fill_len), bfill_at=flat(bfill_at), bfill_len=flat(bfill_len))


def _moe_and_final_norm(h1, norm_ffn_g, router_w, router_b, w_gate_up, b_gate_up, w_down, b_down, norm_final_g):
    n, d = h1.shape
    e = router_w.shape[-1]
    tm_tok = _tile_rows(n, 256)
    tm_e = _tile_rows(n, 512) // ROW_PACK
    local_rows = -(-(TOP_K * tm_tok + e) // (ROW_PACK * RUN_CHUNK)) * (ROW_PACK * RUN_CHUNK)
    blk_rows = local_rows // ROW_PACK + RUN_CHUNK
    xb, lp, prob, cnt3 = _router(h1, norm_ffn_g, router_w, router_b, tm=tm_tok)
    s = _moe_schedule(cnt3[:, :, 0], tm_tok, tm_e, blk_rows)
    xs = _dispatch(xb, lp, s["cnt"], s["loff"], s["dst"], s["fill_at"], s["fill_len"], s["n_tiles"] * tm_e,
                   local_rows // ROW_PACK, tm=tm_tok, n_experts=e)
    sched = (s["tile_expert"], s["n_used"], s["tile_j"], s["ilo"], s["ihi"], s["cnt"], s["loff"], s["pre"],
             s["bfill_at"], s["bfill_len"])
    ys = _experts(xs, sched, w_gate_up, b_gate_up, w_down, b_down, n // tm_tok, blk_rows, tm=tm_e)
    return _combine(h1, ys, lp.T, prob.T, norm_final_g, blk_rows, tm=tm_tok)


def kernel(x, meta_tokens, norm_mix_g, w_in, ssm_a_re, ssm_a_im, ssm_log_dt, ssm_b_re, ssm_b_im, ssm_c_re, ssm_c_im, ssm_d, w_s5_lin, w_s5_gate, conv_dw_w, conv_dw_b, conv_ln_g, conv_ln_b, w_conv_out, w_out, norm_ffn_g, router_w, router_b, w_gate_up, b_gate_up, w_down, b_down, norm_final_g):
    assert norm_mix_g.shape[0] == 1, "single-layer block"
    bsz, seq, d_model = x.shape
    h1 = _mixer(x, meta_tokens, norm_mix_g[0], w_in[0], ssm_a_re[0], ssm_a_im[0], ssm_log_dt[0], ssm_b_re[0],
                ssm_b_im[0], ssm_c_re[0], ssm_c_im[0], ssm_d[0], w_s5_lin[0], w_s5_gate[0], conv_dw_w[0],
                conv_dw_b[0], conv_ln_g[0], conv_ln_b[0], w_conv_out[0], w_out[0], t_chunk=32)
    out = _moe_and_final_norm(h1.reshape(bsz * seq, d_model), norm_ffn_g[0], router_w[0], router_b[0],
                              w_gate_up[0], b_gate_up[0], w_down[0], b_down[0], norm_final_g)
    return out.reshape(bsz, seq, d_model).astype(x.dtype)
```

```python
import functools
import math

import jax
import jax.numpy as jnp
from jax import lax
from jax.experimental import pallas as pl
from jax.experimental.pallas import tpu as pltpu

RMS_EPS = 1e-6
LN_EPS = 1e-5
TOP_K = 4
SWIGLU_LIMIT = 7.0
SWIGLU_ALPHA = 1.702

LANES = 128
SUBLANES = 8
VMEM_LIMIT = 56 * 1024 * 1024
SCAN_CHAINS = 8
CONV_OUT_TILE = 8


def _sigmoid(v):
    return 1.0 / (1.0 + jnp.exp(-v))


def _const_spec(shape):
    nd = len(shape)
    return pl.BlockSpec(shape, lambda *_: (0,) * nd)


def _mixer_kernel(x_ref, meta_ref, g_ref, win_ref, ar_ref, ai_ref, bd_ref, ct_ref, dskip_ref,
                  wlin_ref, wgate_ref, dww_ref, dwb_ref, lng_ref, lnb_ref, wco_ref, wout_ref,
                  out_ref,
                  xs_ref, st_ref, state_ref, cbuf_ref, conv_ref,
                  *, t_chunk, d_ssm, d_conv, conv_w, sb_lanes):
    i = pl.program_id(0)
    bsz = SUBLANES
    rows = t_chunk * bsz
    d_model = xs_ref.shape[1]
    n_sb = d_ssm // LANES
    hist = (conv_w - 1) * bsz

    @pl.when(i == 0)
    def _():
        state_ref[...] = jnp.zeros_like(state_ref)
        cbuf_ref[...] = jnp.zeros_like(cbuf_ref)
        for t in range(t_chunk):
            xs_ref[t * bsz:(t + 1) * bsz, :] = jnp.broadcast_to(meta_ref[t:t + 1, :], (bsz, d_model))

    @pl.when(i > 0)
    def _():
        for t in range(t_chunk):
            xs_ref[t * bsz:(t + 1) * bsz, :] = x_ref[:, pl.ds(t, 1), :].reshape(bsz, d_model)

    xs = xs_ref[...]
    hn = xs * lax.rsqrt(jnp.mean(xs * xs, axis=-1, keepdims=True) + RMS_EPS) * g_ref[...]
    hn_bf = hn.astype(jnp.bfloat16)
    s1 = d_ssm
    s2 = s1 + 2 * d_conv
    in_proj = lambda lo, hi: jnp.dot(hn_bf, win_ref[:, lo:hi], preferred_element_type=jnp.float32)

    zc = in_proj(s1, s2)
    cbuf_ref[hist:hist + rows, :] = zc[:, :d_conv] * _sigmoid(zc[:, d_conv:])

    u = in_proj(0, s1)
    u_bf = u.astype(jnp.bfloat16)
    for sb in range(n_sb):
        st_ref[:, sb * 2 * sb_lanes:(sb + 1) * 2 * sb_lanes] = jnp.dot(
            u_bf[:, sb * LANES:(sb + 1) * LANES], bd_ref[sb], preferred_element_type=jnp.float32)

    for lb in range(d_conv // LANES):
        ls = slice(lb * LANES, (lb + 1) * LANES)
        wk = [jnp.broadcast_to(dww_ref[k:k + 1, ls], (bsz, LANES)) for k in range(conv_w)]
        for t0 in range(0, t_chunk, CONV_OUT_TILE):
            acc = [None] * CONV_OUT_TILE
            for tt in range(t0, t0 + CONV_OUT_TILE + conv_w - 1):
                v = cbuf_ref[tt * bsz:(tt + 1) * bsz, ls]
                for o in range(CONV_OUT_TILE):
                    k = tt - (t0 + o)
                    if 0 <= k < conv_w:
                        acc[o] = wk[k] * v if acc[o] is None else acc[o] + wk[k] * v
            for o in range(CONV_OUT_TILE):
                conv_ref[(t0 + o) * bsz:(t0 + o + 1) * bsz, ls] = acc[o]
    cbuf_ref[0:hist, :] = cbuf_ref[rows:rows + hist, :]

    zg = in_proj(s2, win_ref.shape[1])

    n_lb = sb_lanes // LANES
    blocks = []
    for sb in range(n_sb):
        base = sb * 2 * sb_lanes
        for j in range(n_lb):
            blocks.append((slice(base + j * LANES, base + (j + 1) * LANES),
                           slice(base + sb_lanes + j * LANES, base + sb_lanes + (j + 1) * LANES),
                           slice(sb * sb_lanes + j * LANES, sb * sb_lanes + (j + 1) * LANES)))
    for g0 in range(0, len(blocks), SCAN_CHAINS):
        grp = blocks[g0:g0 + SCAN_CHAINS]
        ar = [jnp.broadcast_to(ar_ref[:, la], (bsz, LANES)) for _, _, la in grp]
        ai = [jnp.broadcast_to(ai_ref[:, la], (bsz, LANES)) for _, _, la in grp]
        sre = [state_ref[:, lre] for lre, _, _ in grp]
        sim = [state_ref[:, lim] for _, lim, _ in grp]
        for t in range(t_chunk):
            r = slice(t * bsz, (t + 1) * bsz)
            for c, (lre, lim, _) in enumerate(grp):
                nre = ar[c] * sre[c] - ai[c] * sim[c] + st_ref[r, lre]
                nim = ar[c] * sim[c] + ai[c] * sre[c] + st_ref[r, lim]
                st_ref[r, lre] = nre
                st_ref[r, lim] = nim
                sre[c], sim[c] = nre, nim
        for c, (lre, lim, _) in enumerate(grp):
            state_ref[:, lre] = sre[c]
            state_ref[:, lim] = sim[c]
    ys = []
    for sb in range(n_sb):
        ys.append(jnp.dot(st_ref[:, sb * 2 * sb_lanes:(sb + 1) * 2 * sb_lanes].astype(jnp.bfloat16),
                          ct_ref[sb], preferred_element_type=jnp.float32))
    y = jnp.concatenate(ys, axis=-1) if n_sb > 1 else ys[0]
    y = y + dskip_ref[...] * u
    ya = 0.5 * y * (1.0 + lax.erf(y * (1.0 / math.sqrt(2.0))))
    ya_bf = ya.astype(jnp.bfloat16)
    ya2 = (jnp.dot(ya_bf, wlin_ref[...], preferred_element_type=jnp.float32)
           * _sigmoid(jnp.dot(ya_bf, wgate_ref[...], preferred_element_type=jnp.float32)))

    c = conv_ref[...] + dwb_ref[...]
    mu = jnp.mean(c, axis=-1, keepdims=True)
    cc = c - mu
    var = jnp.mean(cc * cc, axis=-1, keepdims=True)
    c = cc * lax.rsqrt(var + LN_EPS) * lng_ref[...] + lnb_ref[...]
    c = c * _sigmoid(c)
    yb = jnp.dot(c.astype(jnp.bfloat16), wco_ref[...], preferred_element_type=jnp.float32)

    merged = _sigmoid(zg[:, :d_model]) * ya2 + _sigmoid(zg[:, d_model:]) * yb
    h1 = xs + jnp.dot(merged.astype(jnp.bfloat16), wout_ref[...], preferred_element_type=jnp.float32)
    xs_ref[...] = h1
    for t in range(t_chunk):
        out_ref[:, pl.ds(t, 1), :] = xs_ref[t * bsz:(t + 1) * bsz, :].reshape(bsz, 1, d_model)


def _s5_discretize(a_re, a_im, log_dt, b_re, b_im, c_re, c_im):
    f32 = jnp.float32
    g, p = a_re.shape
    h = b_re.shape[-1]
    gps = LANES // h
    n_sb = g // gps
    dt = jnp.exp(log_dt.astype(f32))[:, None]
    decay = jnp.exp(a_re * dt)
    abar_re = decay * jnp.cos(a_im * dt)
    abar_im = decay * jnp.sin(a_im * dt)
    den = a_re * a_re + a_im * a_im
    nr = abar_re - 1.0
    coef_re = (nr * a_re + abar_im * a_im) / den
    coef_im = (abar_im * a_re - nr * a_im) / den
    bd_re = coef_re[..., None] * b_re - coef_im[..., None] * b_im
    bd_im = coef_re[..., None] * b_im + coef_im[..., None] * b_re
    eye = jnp.eye(gps, dtype=f32)

    def blockdiag_in(w):
        w = w.reshape(n_sb, gps, p, h)
        return jnp.einsum('sgph,gk->sghkp', w, eye).reshape(n_sb, gps * h, gps * p)

    def blockdiag_out(w):
        w = w.reshape(n_sb, gps, h, p)
        return jnp.einsum('sghp,gk->skpgh', w, eye).reshape(n_sb, gps * p, gps * h)

    bd = jnp.concatenate([blockdiag_in(bd_re), blockdiag_in(bd_im)], axis=-1)
    ct = jnp.concatenate([blockdiag_out(c_re), -blockdiag_out(c_im)], axis=1)
    return (abar_re.reshape(1, g * p), abar_im.reshape(1, g * p),
            bd.astype(jnp.bfloat16), ct.astype(jnp.bfloat16), gps * p)


def _mixer(x, meta_tokens, norm_g, w_in, a_re, a_im, log_dt, b_re, b_im, c_re, c_im, d_skip,
           w_s5_lin, w_s5_gate, dw_w, dw_b, ln_g, ln_b, w_conv_out, w_out, *, t_chunk):
    bsz, seq, d_model = x.shape
    n_meta = meta_tokens.shape[0]
    g, p = a_re.shape
    d_ssm = g * b_re.shape[-1]
    conv_w, _, d_conv = dw_w.shape
    assert bsz == SUBLANES and seq % t_chunk == 0 and t_chunk >= n_meta and t_chunk % SUBLANES == 0
    assert d_ssm % LANES == 0 and d_conv % LANES == 0
    f32, bf16 = jnp.float32, jnp.bfloat16
    ar, ai, bd, ct, sb_lanes = _s5_discretize(a_re.astype(f32), a_im.astype(f32), log_dt, b_re.astype(f32),
                                              b_im.astype(f32), c_re.astype(f32), c_im.astype(f32))
    meta_chunk = jnp.concatenate([jnp.zeros((t_chunk - n_meta, d_model), f32), meta_tokens.astype(f32)], axis=0)
    rows = t_chunk * bsz
    n_state = g * p
    consts = [
        meta_chunk, norm_g.reshape(1, d_model).astype(f32), w_in.astype(bf16), ar, ai, bd, ct,
        d_skip.reshape(1, d_ssm).astype(f32), w_s5_lin.astype(bf16), w_s5_gate.astype(bf16),
        dw_w.reshape(conv_w, d_conv).astype(f32), dw_b.reshape(1, d_conv).astype(f32),
        ln_g.reshape(1, d_conv).astype(f32), ln_b.reshape(1, d_conv).astype(f32),
        w_conv_out.astype(bf16), w_out.astype(bf16),
    ]
    x_spec = pl.BlockSpec((bsz, t_chunk, d_model), lambda i: (0, jnp.maximum(i - 1, 0), 0))
    kern = functools.partial(_mixer_kernel, t_chunk=t_chunk, d_ssm=d_ssm, d_conv=d_conv, conv_w=conv_w,
                             sb_lanes=sb_lanes)
    return pl.pallas_call(
        kern,
        out_shape=jax.ShapeDtypeStruct((bsz, seq, d_model), f32),
        grid=(seq // t_chunk + 1,),
        in_specs=[x_spec] + [_const_spec(c.shape) for c in consts],
        out_specs=x_spec,
        scratch_shapes=[
            pltpu.VMEM((rows, d_model), f32),
            pltpu.VMEM((rows, 2 * n_state), f32),
            pltpu.VMEM((bsz, 2 * n_state), f32),
            pltpu.VMEM(((conv_w - 1) * bsz + rows, d_conv), f32),
            pltpu.VMEM((rows, d_conv), f32),
        ],
        compiler_params=pltpu.CompilerParams(dimension_semantics=("arbitrary",), vmem_limit_bytes=VMEM_LIMIT),
        name="mixer",
    )(x.astype(f32), *consts)


ROW_PACK = 2
RUN_CHUNK = 32
WAIT_GROUP = 8
TILES_PER_STEP = 4


def _router_kernel(h_ref, g_ref, rwh_ref, rwl_ref, rb_ref, xb_ref, lp_ref, prob_ref, cnt_ref, *, n_experts, tm):
    for s in range(h_ref.shape[0] // tm):
        rows = slice(s * tm, (s + 1) * tm)
        hn_hi, lps, probs, pairs = _route_tile(h_ref[rows, :], g_ref[...], rwh_ref[...], rwl_ref[...], rb_ref[...],
                                               n_experts)
        xb_ref[rows, :] = hn_hi
        for k in range(TOP_K):
            prob_ref[k:k + 1, rows] = probs[k]
            lp_ref[k:k + 1, rows] = lps[k]
        cnt_ref[s] = pairs


def _route_tile(h, g, rwh, rwl, rb, n_experts):
    tm = h.shape[0]
    hn = h * lax.rsqrt(jnp.mean(h * h, axis=-1, keepdims=True) + RMS_EPS) * g
    hn_hi = hn.astype(jnp.bfloat16)
    hn_lo = (hn - hn_hi.astype(jnp.float32)).astype(jnp.bfloat16)
    nt = (((1,), (1,)), ((), ()))
    dot = lambda a, b: lax.dot_general(a, b, nt, preferred_element_type=jnp.float32)
    lg = dot(rwh, hn_hi) + dot(rwh, hn_lo) + dot(rwl, hn_hi) + rb
    eidx = lax.broadcasted_iota(jnp.int32, (n_experts, tm), 0)
    work = lg
    vals, hots = [], []
    for _ in range(TOP_K):
        m = jnp.max(work, axis=0, keepdims=True)
        sel = jnp.min(jnp.where(work == m, eidx, n_experts), axis=0, keepdims=True)
        hot = eidx == sel
        vals.append(m)
        hots.append(hot)
        work = jnp.where(hot, -jnp.inf, work)
    exps = [jnp.exp(v - vals[0]) for v in vals]
    tot = exps[0]
    for e in exps[1:]:
        tot = tot + e
    inv = 1.0 / tot
    oh = hots[0].astype(jnp.float32)
    for hot in hots[1:]:
        oh = oh + hot.astype(jnp.float32)
    upper = (lax.broadcasted_iota(jnp.int32, (tm, tm), 0) < lax.broadcasted_iota(jnp.int32, (tm, tm), 1))
    cum = jnp.dot(oh.astype(jnp.bfloat16), upper.astype(jnp.bfloat16), preferred_element_type=jnp.float32)
    cnt = jnp.sum(oh, axis=1, keepdims=True)
    lower = (lax.broadcasted_iota(jnp.int32, (n_experts, n_experts), 1)
             < lax.broadcasted_iota(jnp.int32, (n_experts, n_experts), 0))
    cnt_b = jnp.broadcast_to(cnt, (n_experts, LANES))
    pairs = jnp.floor((cnt_b + 1.0) * 0.5)
    cnt_b = 2.0 * pairs
    c_hi = jnp.floor(cnt_b * (1.0 / 16.0))
    c_lo = cnt_b - 16.0 * c_hi
    ldot = lambda v: jnp.dot(lower.astype(jnp.bfloat16), v.astype(jnp.bfloat16), preferred_element_type=jnp.float32)
    loff = 16.0 * ldot(c_hi) + ldot(c_lo)
    base = cum + loff[:, 0:1]
    probs = [ex * inv for ex in exps]
    lps = [jnp.sum(jnp.where(hot, base, 0.0), axis=0, keepdims=True).astype(jnp.int32) for hot in hots]
    return hn_hi, lps, probs, pairs.astype(jnp.int32)


def _router(h1, norm_g, router_w, router_b, *, tm):
    n, d = h1.shape
    e = router_w.shape[-1]
    f32 = jnp.float32
    rwt = router_w.astype(f32).T
    rwh = rwt.astype(jnp.bfloat16)
    rwl = (rwt - rwh.astype(f32)).astype(jnp.bfloat16)
    kn = lambda dt: jax.ShapeDtypeStruct((TOP_K, n), dt)
    sub = 1
    blk = sub * tm
    kspec = pl.BlockSpec((TOP_K, blk), lambda i: (0, i))
    return pl.pallas_call(
        functools.partial(_router_kernel, n_experts=e, tm=tm),
        out_shape=(jax.ShapeDtypeStruct((n, d), jnp.bfloat16), kn(jnp.int32), kn(f32),
                   jax.ShapeDtypeStruct((n // tm, e, LANES), jnp.int32)),
        grid=(n // blk,),
        in_specs=[pl.BlockSpec((blk, d), lambda i: (i, 0)), _const_spec((1, d)), _const_spec((e, d)),
                  _const_spec((e, d)), _const_spec((e, 1))],
        out_specs=(pl.BlockSpec((blk, d), lambda i: (i, 0)), kspec, kspec,
                   pl.BlockSpec((sub, e, LANES), lambda i: (i, 0, 0))),
        compiler_params=pltpu.CompilerParams(dimension_semantics=("arbitrary",), vmem_limit_bytes=VMEM_LIMIT),
        name="router",
    )(h1, norm_g.reshape(1, d).astype(f32), rwh, rwl, router_b.reshape(e, 1).astype(f32))


def _sort_matrix(lp, n_rows, axis):
    if axis == 0:
        shape = (n_rows, lp.shape[1])
        hit = lambda k: lax.broadcasted_iota(jnp.int32, shape, 0) == lp[k:k + 1, :]
    else:
        shape = (lp.shape[0], n_rows)
        hit = lambda k: lax.broadcasted_iota(jnp.int32, shape, 1) == lp[:, k:k + 1]
    m = hit(0)
    for k in range(1, TOP_K):
        m = m | hit(k)
    return m


def _start_runs(n_runs, run_args, src_ref, dst_ref, sem, src_step=RUN_CHUNK):
    def one_run(r, started):
        s, d, ln = run_args(r)
        n_ch = (jnp.maximum(ln, 0) + (RUN_CHUNK - 1)) // RUN_CHUNK

        def one_chunk(q, c):
            pltpu.make_async_copy(src_ref.at[pl.ds(s + q * src_step, RUN_CHUNK)],
                                  dst_ref.at[pl.ds(d + q * RUN_CHUNK, RUN_CHUNK)], sem).start()
            return c

        lax.fori_loop(0, n_ch, one_chunk, 0)
        return started + n_ch

    return lax.fori_loop(0, n_runs, one_run, jnp.int32(0))


def _drain(count, src_ref, dst_ref, sem):
    def wait_chunks(n_chunks):
        def body(q, c):
            pltpu.make_async_copy(src_ref.at[pl.ds(0, n_chunks * RUN_CHUNK)],
                                  dst_ref.at[pl.ds(0, n_chunks * RUN_CHUNK)], sem).wait()
            return c
        return body

    groups = count // WAIT_GROUP
    lax.fori_loop(0, groups, wait_chunks(WAIT_GROUP), 0)
    lax.fori_loop(0, count - groups * WAIT_GROUP, wait_chunks(1), 0)


def _dispatch_kernel(cnt_ref, loff_ref, dst_ref, fill_at_ref, fill_len_ref, xb_ref, lp_ref, xs_ref,
                     xl_ref, pend_ref, sem, *, n_experts):
    i = pl.program_id(0)
    rows = xl_ref.shape[1] - RUN_CHUNK
    d = xl_ref.shape[3]

    @pl.when(i == 0)
    def _():
        for s in range(2):
            xl_ref[s, rows:] = jnp.zeros((RUN_CHUNK, ROW_PACK, d), xl_ref.dtype)
        n_fill = _start_runs(n_experts + 1, lambda r: (rows, fill_at_ref[r], fill_len_ref[r]), xl_ref.at[0], xs_ref,
                             sem, src_step=0)
        _drain(n_fill, xl_ref.at[0], xs_ref, sem)
        pend_ref[0] = 0

    src = xl_ref.at[i % 2]
    perm = _sort_matrix(lp_ref[...], rows * ROW_PACK, 0).astype(jnp.bfloat16)
    sorted_rows = jnp.dot(perm, xb_ref[...], preferred_element_type=jnp.float32).astype(jnp.bfloat16)
    src[:rows] = sorted_rows.reshape(rows, ROW_PACK, d)

    def run_args(e):
        r = i * n_experts + e
        return loff_ref[r], dst_ref[r], cnt_ref[r]

    _drain(pend_ref[0], src, xs_ref, sem)
    pend_ref[0] = _start_runs(n_experts, run_args, src, xs_ref, sem)

    @pl.when(i == pl.num_programs(0) - 1)
    def _():
        _drain(pend_ref[0], src, xs_ref, sem)


def _dispatch(xb, lp, cnt_flat, loff_flat, dst_flat, fill_at, fill_len, m_pad, local_pairs, *, tm, n_experts):
    n, d = xb.shape
    grid_spec = pltpu.PrefetchScalarGridSpec(
        num_scalar_prefetch=5,
        grid=(n // tm,),
        in_specs=[pl.BlockSpec((tm, d), lambda i, *_: (i, 0)), pl.BlockSpec((TOP_K, tm), lambda i, *_: (0, i))],
        out_specs=pl.BlockSpec(memory_space=pl.ANY),
        scratch_shapes=[pltpu.VMEM((2, local_pairs + RUN_CHUNK, ROW_PACK, d), jnp.bfloat16),
                        pltpu.SMEM((1,), jnp.int32), pltpu.SemaphoreType.DMA],
    )
    return pl.pallas_call(
        functools.partial(_dispatch_kernel, n_experts=n_experts),
        out_shape=jax.ShapeDtypeStruct((m_pad, ROW_PACK, d), jnp.bfloat16),
        grid_spec=grid_spec,
        compiler_params=pltpu.CompilerParams(dimension_semantics=("arbitrary",), vmem_limit_bytes=VMEM_LIMIT),
        name="dispatch",
    )(cnt_flat, loff_flat, dst_flat, fill_at, fill_len, xb, lp)


def _expert_kernel(te_ref, nu_ref, tj_ref, ilo_ref, ihi_ref, cnt_ref, loff_ref, pre_ref, bfill_at_ref, bfill_len_ref,
                   xs_ref, wgu_ref, bgu_ref, wd_ref, bd_ref, ys_ref, wgu_bf_ref, wd_bf_ref, yb_ref, pend_ref, sem,
                   *, n_experts, blk_rows):
    j = pl.program_id(0)
    tm, _, d = xs_ref.shape

    @pl.when(j == 0)
    def _():
        for s in range(2):
            yb_ref[s, tm:] = jnp.zeros((RUN_CHUNK, ROW_PACK, d), yb_ref.dtype)
        n_blocks = ys_ref.shape[0] // blk_rows
        n_fill = _start_runs(n_blocks, lambda r: (tm, bfill_at_ref[r], bfill_len_ref[r]), yb_ref.at[0],
                             ys_ref, sem, src_step=0)
        _drain(n_fill, yb_ref.at[0], ys_ref, sem)
        pend_ref[0] = 0

    src = yb_ref.at[j % 2]

    @pl.when(j < nu_ref[0])
    def _():
        d_exp = wd_ref.shape[0]

        @pl.when(tj_ref[j] == 0)
        def _():
            wgu_bf_ref[...] = wgu_ref[...].astype(jnp.bfloat16)
            wd_bf_ref[...] = wd_ref[...].astype(jnp.bfloat16)

        x = xs_ref[...].reshape(tm * ROW_PACK, d)
        gu = jnp.dot(x, wgu_bf_ref[...], preferred_element_type=jnp.float32) + bgu_ref[...]
        g = jnp.minimum(gu[:, :d_exp], SWIGLU_LIMIT)
        lin = jnp.clip(gu[:, d_exp:], -SWIGLU_LIMIT, SWIGLU_LIMIT)
        act = g * _sigmoid(SWIGLU_ALPHA * g) * (lin + 1.0)
        y = jnp.dot(act.astype(jnp.bfloat16), wd_bf_ref[...], preferred_element_type=jnp.float32) + bd_ref[...]
        src[:tm] = y.astype(jnp.bfloat16).reshape(tm, ROW_PACK, d)

    _drain(pend_ref[0], src, ys_ref, sem)
    pend_ref[0] = 0

    @pl.when(j < nu_ref[0])
    def _():
        e = te_ref[j]
        base = tj_ref[j] * tm
        i0 = ilo_ref[j]

        def run_args(r):
            it = i0 + r
            a = pre_ref[it * n_experts + e] - base
            b = a + cnt_ref[it * n_experts + e]
            ac = jnp.maximum(a, 0)
            return ac, it * blk_rows + loff_ref[it * n_experts + e] + (ac - a), jnp.minimum(b, tm) - ac

        pend_ref[0] = _start_runs(ihi_ref[j] - i0, run_args, src, ys_ref, sem)

    @pl.when(j == pl.num_programs(0) - 1)
    def _():
        _drain(pend_ref[0], src, ys_ref, sem)


def _experts(xs, sched, w_gate_up, b_gate_up, w_down, b_down, n_tok_tiles, blk_rows, *, tm):
    m_pad = xs.shape[0]
    e, d, d_gu = w_gate_up.shape
    d_exp = w_down.shape[1]
    bf16, f32 = jnp.bfloat16, jnp.float32
    row_map = lambda j, te, nu, *_: (jnp.minimum(j, nu[0] - 1), 0, 0)
    w_map = lambda j, te, *_: (te[j], 0, 0)
    grid_spec = pltpu.PrefetchScalarGridSpec(
        num_scalar_prefetch=len(sched),
        grid=(m_pad // tm,),
        in_specs=[pl.BlockSpec((tm, ROW_PACK, d), row_map),
                  pl.BlockSpec((None, d, d_gu), w_map), pl.BlockSpec((None, 1, d_gu), w_map),
                  pl.BlockSpec((None, d_exp, d), w_map), pl.BlockSpec((None, 1, d), w_map)],
        out_specs=pl.BlockSpec(memory_space=pl.ANY),
        scratch_shapes=[pltpu.VMEM((d, d_gu), bf16), pltpu.VMEM((d_exp, d), bf16),
                        pltpu.VMEM((2, tm + RUN_CHUNK, ROW_PACK, d), bf16),
                        pltpu.SMEM((1,), jnp.int32), pltpu.SemaphoreType.DMA],
    )
    return pl.pallas_call(
        functools.partial(_expert_kernel, n_experts=e, blk_rows=blk_rows),
        out_shape=jax.ShapeDtypeStruct((n_tok_tiles * blk_rows, ROW_PACK, d), bf16),
        grid_spec=grid_spec,
        compiler_params=pltpu.CompilerParams(dimension_semantics=("arbitrary",), vmem_limit_bytes=VMEM_LIMIT),
        name="experts",
    )(*sched, xs, w_gate_up.astype(f32), b_gate_up.reshape(e, 1, d_gu).astype(f32),
      w_down.astype(f32), b_down.reshape(e, 1, d).astype(f32))


def _combine_kernel(h_ref, ys_ref, lpt_ref, pt_ref, g_ref, out_ref, *, tm, blk_rows):
    d = h_ref.shape[1]
    pairs = blk_rows - RUN_CHUNK
    rows = pairs * ROW_PACK
    for s in range(h_ref.shape[0] // tm):
        tok = slice(s * tm, (s + 1) * tm)
        y = ys_ref[s * blk_rows:s * blk_rows + pairs].reshape(rows, d)
        col = lax.broadcasted_iota(jnp.int32, (tm, rows), 1)
        w = jnp.zeros((tm, rows), jnp.float32)
        for k in range(TOP_K):
            w = jnp.where(col == lpt_ref[tok, k:k + 1], pt_ref[tok, k:k + 1], w)
        w_hi = w.astype(jnp.bfloat16)
        w_lo = (w - w_hi.astype(jnp.float32)).astype(jnp.bfloat16)
        acc = (h_ref[tok, :] + jnp.dot(w_hi, y, preferred_element_type=jnp.float32)
               + jnp.dot(w_lo, y, preferred_element_type=jnp.float32))
        out_ref[tok, :] = acc * lax.rsqrt(jnp.mean(acc * acc, axis=-1, keepdims=True) + RMS_EPS) * g_ref[...]


def _combine(h1, ys, lpt, prob_t, norm_g, blk_rows, *, tm):
    n, d = h1.shape
    f32 = jnp.float32
    sub = TILES_PER_STEP if (n // tm) % TILES_PER_STEP == 0 else 1
    blk = sub * tm
    tspec = pl.BlockSpec((blk, TOP_K), lambda i: (i, 0))
    return pl.pallas_call(
        functools.partial(_combine_kernel, tm=tm, blk_rows=blk_rows),
        out_shape=jax.ShapeDtypeStruct((n, d), f32),
        grid=(n // blk,),
        in_specs=[pl.BlockSpec((blk, d), lambda i: (i, 0)),
                  pl.BlockSpec((sub * blk_rows, ROW_PACK, d), lambda i: (i, 0, 0)),
                  tspec, tspec, _const_spec((1, d))],
        out_specs=pl.BlockSpec((blk, d), lambda i: (i, 0)),
        compiler_params=pltpu.CompilerParams(dimension_semantics=("arbitrary",), vmem_limit_bytes=VMEM_LIMIT),
        name="combine",
    )(h1, ys, lpt, prob_t, norm_g.reshape(1, d).astype(f32))


def _tile_rows(n, target):
    t = min(target, n)
    while n % t:
        t //= 2
    return t


def _moe_schedule(cnt, tm_tok, tm_e, blk_rows):
    n_t, e = cnt.shape
    i32 = jnp.int32
    total = jnp.sum(cnt, axis=0)
    tiles_per_e = (total + RUN_CHUNK + tm_e - 1) // tm_e
    tile_end = jnp.cumsum(tiles_per_e)
    tile_begin = tile_end - tiles_per_e
    n_used = tile_end[-1:]
    n_tiles = ((TOP_K * tm_tok + e) * n_t) // (ROW_PACK * tm_e) + 2 * e + 1
    tile_id = jnp.minimum(jnp.arange(n_tiles, dtype=i32), n_used[0] - 1)
    tile_expert = jnp.sum(tile_id[:, None] >= tile_end[None, :], axis=1).astype(i32)
    tile_j = tile_id - tile_begin[tile_expert]
    pre = jnp.cumsum(cnt, axis=0) - cnt
    loff = jnp.cumsum(cnt, axis=1) - cnt
    dst = tile_begin[None, :] * tm_e + pre
    pre_t = pre[:, tile_expert].T
    end_t = pre_t + cnt[:, tile_expert].T
    ilo = jnp.sum(end_t <= (tile_j * tm_e)[:, None], axis=1).astype(i32)
    ihi = jnp.sum(pre_t < ((tile_j + 1) * tm_e)[:, None], axis=1).astype(i32)
    gap = tiles_per_e * tm_e - total
    fill_len = (gap + RUN_CHUNK - 1) // RUN_CHUNK * RUN_CHUNK
    fill_at = jnp.concatenate([tile_end * tm_e - fill_len, n_used * tm_e])
    fill_len = jnp.concatenate([fill_len, (n_tiles - n_used) * tm_e])
    bfill_len = (blk_rows - jnp.sum(cnt, axis=1) + RUN_CHUNK - 1) // RUN_CHUNK * RUN_CHUNK
    bfill_at = (jnp.arange(n_t, dtype=i32) + 1) * blk_rows - bfill_len
    flat = lambda a: a.reshape(-1).astype(i32)
    return dict(n_tiles=n_tiles, tile_expert=tile_expert, n_used=n_used.astype(i32), tile_j=tile_j.astype(i32),
                ilo=ilo, ihi=ihi, cnt=flat(cnt), loff=flat(loff), pre=flat(pre), dst=flat(dst),
                fill_at=flat(fill_at), fill_len=flat(fill_len), bfill_at=flat(bfill_at), bfill_len=flat(bfill_len))


def _moe_and_final_norm(h1, norm_ffn_g, router_w, router_b, w_gate_up, b_gate_up, w_down, b_down, norm_final_g):
    n, d = h1.shape
    e = router_w.shape[-1]
    tm_tok = _tile_rows(n, 256)
    tm_e = _tile_rows(n, 512) // ROW_PACK
    local_rows = -(-(TOP_K * tm_tok + e) // (ROW_PACK * RUN_CHUNK)) * (ROW_PACK * RUN_CHUNK)
    blk_rows = local_rows // ROW_PACK + RUN_CHUNK
    xb, lp, prob, cnt3 = _router(h1, norm_ffn_g, router_w, router_b, tm=tm_tok)
    s = _moe_schedule(cnt3[:, :, 0], tm_tok, tm_e, blk_rows)
    xs = _dispatch(xb, lp, s["cnt"], s["loff"], s["dst"], s["fill_at"], s["fill_len"], s["n_tiles"] * tm_e,
                   local_rows // ROW_PACK, tm=tm_tok, n_experts=e)
    sched = (s["tile_expert"], s["n_used"], s["tile_j"], s["ilo"], s["ihi"], s["cnt"], s["loff"], s["pre"],
             s["bfill_at"], s["bfill_len"])
    ys = _experts(xs, sched, w_gate_up, b_gate_up, w_down, b_down, n // tm_tok, blk_rows, tm=tm_e)
    return _combine(h1, ys, lp.T, prob.T, norm_final_g, blk_rows, tm=tm_tok)


def kernel(x, meta_tokens, norm_mix_g, w_in, ssm_a_re, ssm_a_im, ssm_log_dt, ssm_b_re, ssm_b_im, ssm_c_re, ssm_c_im, ssm_d, w_s5_lin, w_s5_gate, conv_dw_w, conv_dw_b, conv_ln_g, conv_ln_b, w_conv_out, w_out, norm_ffn_g, router_w, router_b, w_gate_up, b_gate_up, w_down, b_down, norm_final_g):
    assert norm_mix_g.shape[0] == 1, "single-layer block"
    bsz, seq, d_model = x.shape
    h1 = _mixer(x, meta_tokens, norm_mix_g[0], w_in[0], ssm_a_re[0], ssm_a_im[0], ssm_log_dt[0], ssm_b_re[0],
                ssm_b_im[0], ssm_c_re[0], ssm_c_im[0], ssm_d[0], w_s5_lin[0], w_s5_gate[0], conv_dw_w[0],
                conv_dw_b[0], conv_ln_g[0], conv_ln_b[0], w_conv_out[0], w_out[0], t_chunk=32)
    out = _moe_and_final_norm(h1.reshape(bsz * seq, d_model), norm_ffn_g[0], router_w[0], router_b[0],
                              w_gate_up[0], b_gate_up[0], w_down[0], b_down[0], norm_final_g)
    return out.reshape(bsz, seq, d_model).astype(x.dtype)
```

```python
import functools
import math

import jax
import jax.numpy as jnp
from jax import lax
from jax.experimental import pallas as pl
from jax.experimental.pallas import tpu as pltpu

RMS_EPS = 1e-6
LN_EPS = 1e-5
TOP_K = 4
SWIGLU_LIMIT = 7.0
SWIGLU_ALPHA = 1.702

LANES = 128
SUBLANES = 8
VMEM_LIMIT = 56 * 1024 * 1024
SCAN_CHAINS = 8
CONV_OUT_TILE = 8
MIXER_T_CHUNK = 64


def _sigmoid(v):
    return 1.0 / (1.0 + jnp.exp(-v))


def _const_spec(shape):
    nd = len(shape)
    return pl.BlockSpec(shape, lambda *_: (0,) * nd)


def _mixer_kernel(x_ref, meta_ref, g_ref, win_ref, ar_ref, ai_ref, bd_ref, ct_ref, dskip_ref,
                  wlin_ref, wgate_ref, dww_ref, dwb_ref, lng_ref, lnb_ref, wco_ref, wout_ref,
                  out_ref,
                  xs_ref, st_ref, state_ref, cbuf_ref, conv_ref,
                  *, t_chunk, d_ssm, d_conv, conv_w, sb_lanes):
    i = pl.program_id(0)
    bsz = SUBLANES
    rows = t_chunk * bsz
    d_model = xs_ref.shape[1]
    n_sb = d_ssm // LANES
    hist = (conv_w - 1) * bsz

    @pl.when(i == 0)
    def _():
        state_ref[...] = jnp.zeros_like(state_ref)
        cbuf_ref[...] = jnp.zeros_like(cbuf_ref)
        for t in range(t_chunk):
            xs_ref[t * bsz:(t + 1) * bsz, :] = jnp.broadcast_to(meta_ref[t:t + 1, :], (bsz, d_model))

    @pl.when(i > 0)
    def _():
        for t in range(t_chunk):
            xs_ref[t * bsz:(t + 1) * bsz, :] = x_ref[:, pl.ds(t, 1), :].reshape(bsz, d_model)

    xs = xs_ref[...]
    hn = xs * lax.rsqrt(jnp.mean(xs * xs, axis=-1, keepdims=True) + RMS_EPS) * g_ref[...]
    hn_bf = hn.astype(jnp.bfloat16)
    s1 = d_ssm
    s2 = s1 + 2 * d_conv
    in_proj = lambda lo, hi: jnp.dot(hn_bf, win_ref[:, lo:hi], preferred_element_type=jnp.float32)

    zc = in_proj(s1, s2)
    cbuf_ref[hist:hist + rows, :] = zc[:, :d_conv] * _sigmoid(zc[:, d_conv:])

    u = in_proj(0, s1)
    u_bf = u.astype(jnp.bfloat16)
    for sb in range(n_sb):
        st_ref[:, sb * 2 * sb_lanes:(sb + 1) * 2 * sb_lanes] = jnp.dot(
            u_bf[:, sb * LANES:(sb + 1) * LANES], bd_ref[sb], preferred_element_type=jnp.float32)

    for lb in range(d_conv // LANES):
        ls = slice(lb * LANES, (lb + 1) * LANES)
        wk = [jnp.broadcast_to(dww_ref[k:k + 1, ls], (bsz, LANES)) for k in range(conv_w)]
        for t0 in range(0, t_chunk, CONV_OUT_TILE):
            acc = [None] * CONV_OUT_TILE
            for tt in range(t0, t0 + CONV_OUT_TILE + conv_w - 1):
                v = cbuf_ref[tt * bsz:(tt + 1) * bsz, ls]
                for o in range(CONV_OUT_TILE):
                    k = tt - (t0 + o)
                    if 0 <= k < conv_w:
                        acc[o] = wk[k] * v if acc[o] is None else acc[o] + wk[k] * v
            for o in range(CONV_OUT_TILE):
                conv_ref[(t0 + o) * bsz:(t0 + o + 1) * bsz, ls] = acc[o]
    cbuf_ref[0:hist, :] = cbuf_ref[rows:rows + hist, :]

    zg = in_proj(s2, win_ref.shape[1])

    n_lb = sb_lanes // LANES
    blocks = []
    for sb in range(n_sb):
        base = sb * 2 * sb_lanes
        for j in range(n_lb):
            blocks.append((slice(base + j * LANES, base + (j + 1) * LANES),
                           slice(base + sb_lanes + j * LANES, base + sb_lanes + (j + 1) * LANES),
                           slice(sb * sb_lanes + j * LANES, sb * sb_lanes + (j + 1) * LANES)))
    for g0 in range(0, len(blocks), SCAN_CHAINS):
        grp = blocks[g0:g0 + SCAN_CHAINS]
        ar = [jnp.broadcast_to(ar_ref[:, la], (bsz, LANES)) for _, _, la in grp]
        ai = [jnp.broadcast_to(ai_ref[:, la], (bsz, LANES)) for _, _, la in grp]
        sre = [state_ref[:, lre] for lre, _, _ in grp]
        sim = [state_ref[:, lim] for _, lim, _ in grp]
        for t in range(t_chunk):
            r = slice(t * bsz, (t + 1) * bsz)
            for c, (lre, lim, _) in enumerate(grp):
                nre = ar[c] * sre[c] - ai[c] * sim[c] + st_ref[r, lre]
                nim = ar[c] * sim[c] + ai[c] * sre[c] + st_ref[r, lim]
                st_ref[r, lre] = nre
                st_ref[r, lim] = nim
                sre[c], sim[c] = nre, nim
        for c, (lre, lim, _) in enumerate(grp):
            state_ref[:, lre] = sre[c]
            state_ref[:, lim] = sim[c]
    ys = []
    for sb in range(n_sb):
        ys.append(jnp.dot(st_ref[:, sb * 2 * sb_lanes:(sb + 1) * 2 * sb_lanes].astype(jnp.bfloat16),
                          ct_ref[sb], preferred_element_type=jnp.float32))
    y = jnp.concatenate(ys, axis=-1) if n_sb > 1 else ys[0]
    y = y + dskip_ref[...] * u
    ya = 0.5 * y * (1.0 + lax.erf(y * (1.0 / math.sqrt(2.0))))
    ya_bf = ya.astype(jnp.bfloat16)
    ya2 = (jnp.dot(ya_bf, wlin_ref[...], preferred_element_type=jnp.float32)
           * _sigmoid(jnp.dot(ya_bf, wgate_ref[...], preferred_element_type=jnp.float32)))

    c = conv_ref[...] + dwb_ref[...]
    mu = jnp.mean(c, axis=-1, keepdims=True)
    cc = c - mu
    var = jnp.mean(cc * cc, axis=-1, keepdims=True)
    c = cc * lax.rsqrt(var + LN_EPS) * lng_ref[...] + lnb_ref[...]
    c = c * _sigmoid(c)
    yb = jnp.dot(c.astype(jnp.bfloat16), wco_ref[...], preferred_element_type=jnp.float32)

    merged = _sigmoid(zg[:, :d_model]) * ya2 + _sigmoid(zg[:, d_model:]) * yb
    h1 = xs + jnp.dot(merged.astype(jnp.bfloat16), wout_ref[...], preferred_element_type=jnp.float32)
    xs_ref[...] = h1
    for t in range(t_chunk):
        out_ref[:, pl.ds(t, 1), :] = xs_ref[t * bsz:(t + 1) * bsz, :].reshape(bsz, 1, d_model)


def _s5_discretize(a_re, a_im, log_dt, b_re, b_im, c_re, c_im):
    f32 = jnp.float32
    g, p = a_re.shape
    h = b_re.shape[-1]
    gps = LANES // h
    n_sb = g // gps
    dt = jnp.exp(log_dt.astype(f32))[:, None]
    decay = jnp.exp(a_re * dt)
    abar_re = decay * jnp.cos(a_im * dt)
    abar_im = decay * jnp.sin(a_im * dt)
    den = a_re * a_re + a_im * a_im
    nr = abar_re - 1.0
    coef_re = (nr * a_re + abar_im * a_im) / den
    coef_im = (abar_im * a_re - nr * a_im) / den
    bd_re = coef_re[..., None] * b_re - coef_im[..., None] * b_im
    bd_im = coef_re[..., None] * b_im + coef_im[..., None] * b_re
    eye = jnp.eye(gps, dtype=f32)

    def blockdiag_in(w):
        w = w.reshape(n_sb, gps, p, h)
        return jnp.einsum('sgph,gk->sghkp', w, eye).reshape(n_sb, gps * h, gps * p)

    def blockdiag_out(w):
        w = w.reshape(n_sb, gps, h, p)
        return jnp.einsum('sghp,gk->skpgh', w, eye).reshape(n_sb, gps * p, gps * h)

    bd = jnp.concatenate([blockdiag_in(bd_re), blockdiag_in(bd_im)], axis=-1)
    ct = jnp.concatenate([blockdiag_out(c_re), -blockdiag_out(c_im)], axis=1)
    return (abar_re.reshape(1, g * p), abar_im.reshape(1, g * p),
            bd.astype(jnp.bfloat16), ct.astype(jnp.bfloat16), gps * p)


def _mixer(x, meta_tokens, norm_g, w_in, a_re, a_im, log_dt, b_re, b_im, c_re, c_im, d_skip,
           w_s5_lin, w_s5_gate, dw_w, dw_b, ln_g, ln_b, w_conv_out, w_out, *, t_chunk):
    bsz, seq, d_model = x.shape
    n_meta = meta_tokens.shape[0]
    g, p = a_re.shape
    d_ssm = g * b_re.shape[-1]
    conv_w, _, d_conv = dw_w.shape
    assert bsz == SUBLANES and seq % t_chunk == 0 and t_chunk >= n_meta and t_chunk % SUBLANES == 0
    assert d_ssm % LANES == 0 and d_conv % LANES == 0
    f32, bf16 = jnp.float32, jnp.bfloat16
    ar, ai, bd, ct, sb_lanes = _s5_discretize(a_re.astype(f32), a_im.astype(f32), log_dt, b_re.astype(f32),
                                              b_im.astype(f32), c_re.astype(f32), c_im.astype(f32))
    meta_chunk = jnp.concatenate([jnp.zeros((t_chunk - n_meta, d_model), f32), meta_tokens.astype(f32)], axis=0)
    rows = t_chunk * bsz
    n_state = g * p
    consts = [
        meta_chunk, norm_g.reshape(1, d_model).astype(f32), w_in.astype(bf16), ar, ai, bd, ct,
        d_skip.reshape(1, d_ssm).astype(f32), w_s5_lin.astype(bf16), w_s5_gate.astype(bf16),
        dw_w.reshape(conv_w, d_conv).astype(f32), dw_b.reshape(1, d_conv).astype(f32),
        ln_g.reshape(1, d_conv).astype(f32), ln_b.reshape(1, d_conv).astype(f32),
        w_conv_out.astype(bf16), w_out.astype(bf16),
    ]
    x_spec = pl.BlockSpec((bsz, t_chunk, d_model), lambda i: (0, jnp.maximum(i - 1, 0), 0))
    kern = functools.partial(_mixer_kernel, t_chunk=t_chunk, d_ssm=d_ssm, d_conv=d_conv, conv_w=conv_w,
                             sb_lanes=sb_lanes)
    return pl.pallas_call(
        kern,
        out_shape=jax.ShapeDtypeStruct((bsz, seq, d_model), f32),
        grid=(seq // t_chunk + 1,),
        in_specs=[x_spec] + [_const_spec(c.shape) for c in consts],
        out_specs=x_spec,
        scratch_shapes=[
            pltpu.VMEM((rows, d_model), f32),
            pltpu.VMEM((rows, 2 * n_state), f32),
            pltpu.VMEM((bsz, 2 * n_state), f32),
            pltpu.VMEM(((conv_w - 1) * bsz + rows, d_conv), f32),
            pltpu.VMEM((rows, d_conv), f32),
        ],
        compiler_params=pltpu.CompilerParams(dimension_semantics=("arbitrary",), vmem_limit_bytes=VMEM_LIMIT),
        name="mixer",
    )(x.astype(f32), *consts)


ROW_PACK = 2
RUN_CHUNK = 32
WAIT_GROUP = 8
TILES_PER_STEP = 4


def _router_kernel(h_ref, g_ref, rwh_ref, rwl_ref, rb_ref, xb_ref, lp_ref, prob_ref, cnt_ref, *, n_experts, tm):
    for s in range(h_ref.shape[0] // tm):
        rows = slice(s * tm, (s + 1) * tm)
        hn_hi, lps, probs, pairs = _route_tile(h_ref[rows, :], g_ref[...], rwh_ref[...], rwl_ref[...], rb_ref[...],
                                               n_experts)
        xb_ref[rows, :] = hn_hi
        for k in range(TOP_K):
            prob_ref[k:k + 1, rows] = probs[k]
            lp_ref[k:k + 1, rows] = lps[k]
        cnt_ref[s] = pairs


def _route_tile(h, g, rwh, rwl, rb, n_experts):
    tm = h.shape[0]
    hn = h * lax.rsqrt(jnp.mean(h * h, axis=-1, keepdims=True) + RMS_EPS) * g
    hn_hi = hn.astype(jnp.bfloat16)
    hn_lo = (hn - hn_hi.astype(jnp.float32)).astype(jnp.bfloat16)
    nt = (((1,), (1,)), ((), ()))
    dot = lambda a, b: lax.dot_general(a, b, nt, preferred_element_type=jnp.float32)
    lg = dot(rwh, hn_hi) + dot(rwh, hn_lo) + dot(rwl, hn_hi) + rb
    eidx = lax.broadcasted_iota(jnp.int32, (n_experts, tm), 0)
    work = lg
    vals, hots = [], []
    for _ in range(TOP_K):
        m = jnp.max(work, axis=0, keepdims=True)
        sel = jnp.min(jnp.where(work == m, eidx, n_experts), axis=0, keepdims=True)
        hot = eidx == sel
        vals.append(m)
        hots.append(hot)
        work = jnp.where(hot, -jnp.inf, work)
    exps = [jnp.exp(v - vals[0]) for v in vals]
    tot = exps[0]
    for e in exps[1:]:
        tot = tot + e
    inv = 1.0 / tot
    oh = hots[0].astype(jnp.float32)
    for hot in hots[1:]:
        oh = oh + hot.astype(jnp.float32)
    upper = (lax.broadcasted_iota(jnp.int32, (tm, tm), 0) < lax.broadcasted_iota(jnp.int32, (tm, tm), 1))
    cum = jnp.dot(oh.astype(jnp.bfloat16), upper.astype(jnp.bfloat16), preferred_element_type=jnp.float32)
    cnt = jnp.sum(oh, axis=1, keepdims=True)
    lower = (lax.broadcasted_iota(jnp.int32, (n_experts, n_experts), 1)
             < lax.broadcasted_iota(jnp.int32, (n_experts, n_experts), 0))
    cnt_b = jnp.broadcast_to(cnt, (n_experts, LANES))
    pairs = jnp.floor((cnt_b + 1.0) * 0.5)
    cnt_b = 2.0 * pairs
    c_hi = jnp.floor(cnt_b * (1.0 / 16.0))
    c_lo = cnt_b - 16.0 * c_hi
    ldot = lambda v: jnp.dot(lower.astype(jnp.bfloat16), v.astype(jnp.bfloat16), preferred_element_type=jnp.float32)
    loff = 16.0 * ldot(c_hi) + ldot(c_lo)
    base = cum + loff[:, 0:1]
    probs = [ex * inv for ex in exps]
    lps = [jnp.sum(jnp.where(hot, base, 0.0), axis=0, keepdims=True).astype(jnp.int32) for hot in hots]
    return hn_hi, lps, probs, pairs.astype(jnp.int32)


def _router(h1, norm_g, router_w, router_b, *, tm):
    n, d = h1.shape
    e = router_w.shape[-1]
    f32 = jnp.float32
    rwt = router_w.astype(f32).T
    rwh = rwt.astype(jnp.bfloat16)
    rwl = (rwt - rwh.astype(f32)).astype(jnp.bfloat16)
    kn = lambda dt: jax.ShapeDtypeStruct((TOP_K, n), dt)
    sub = 1
    blk = sub * tm
    kspec = pl.BlockSpec((TOP_K, blk), lambda i: (0, i))
    return pl.pallas_call(
        functools.partial(_router_kernel, n_experts=e, tm=tm),
        out_shape=(jax.ShapeDtypeStruct((n, d), jnp.bfloat16), kn(jnp.int32), kn(f32),
                   jax.ShapeDtypeStruct((n // tm, e, LANES), jnp.int32)),
        grid=(n // blk,),
        in_specs=[pl.BlockSpec((blk, d), lambda i: (i, 0)), _const_spec((1, d)), _const_spec((e, d)),
                  _const_spec((e, d)), _const_spec((e, 1))],
        out_specs=(pl.BlockSpec((blk, d), lambda i: (i, 0)), kspec, kspec,
                   pl.BlockSpec((sub, e, LANES), lambda i: (i, 0, 0))),
        compiler_params=pltpu.CompilerParams(dimension_semantics=("arbitrary",), vmem_limit_bytes=VMEM_LIMIT),
        name="router",
    )(h1, norm_g.reshape(1, d).astype(f32), rwh, rwl, router_b.reshape(e, 1).astype(f32))


def _sort_matrix(lp, n_rows, axis):
    if axis == 0:
        shape = (n_rows, lp.shape[1])
        hit = lambda k: lax.broadcasted_iota(jnp.int32, shape, 0) == lp[k:k + 1, :]
    else:
        shape = (lp.shape[0], n_rows)
        hit = lambda k: lax.broadcasted_iota(jnp.int32, shape, 1) == lp[:, k:k + 1]
    m = hit(0)
    for k in range(1, TOP_K):
        m = m | hit(k)
    return m


def _start_runs(n_runs, run_args, src_ref, dst_ref, sem, src_step=RUN_CHUNK):
    def one_run(r, started):
        s, d, ln = run_args(r)
        n_ch = (jnp.maximum(ln, 0) + (RUN_CHUNK - 1)) // RUN_CHUNK

        def one_chunk(q, c):
            pltpu.make_async_copy(src_ref.at[pl.ds(s + q * src_step, RUN_CHUNK)],
                                  dst_ref.at[pl.ds(d + q * RUN_CHUNK, RUN_CHUNK)], sem).start()
            return c

        lax.fori_loop(0, n_ch, one_chunk, 0)
        return started + n_ch

    return lax.fori_loop(0, n_runs, one_run, jnp.int32(0))


def _drain(count, src_ref, dst_ref, sem):
    def wait_chunks(n_chunks):
        def body(q, c):
            pltpu.make_async_copy(src_ref.at[pl.ds(0, n_chunks * RUN_CHUNK)],
                                  dst_ref.at[pl.ds(0, n_chunks * RUN_CHUNK)], sem).wait()
            return c
        return body

    groups = count // WAIT_GROUP
    lax.fori_loop(0, groups, wait_chunks(WAIT_GROUP), 0)
    lax.fori_loop(0, count - groups * WAIT_GROUP, wait_chunks(1), 0)


def _dispatch_kernel(cnt_ref, loff_ref, dst_ref, fill_at_ref, fill_len_ref, xb_ref, lp_ref, xs_ref,
                     xl_ref, pend_ref, sem, *, n_experts):
    i = pl.program_id(0)
    rows = xl_ref.shape[1] - RUN_CHUNK
    d = xl_ref.shape[3]

    @pl.when(i == 0)
    def _():
        for s in range(2):
            xl_ref[s, rows:] = jnp.zeros((RUN_CHUNK, ROW_PACK, d), xl_ref.dtype)
        n_fill = _start_runs(n_experts + 1, lambda r: (rows, fill_at_ref[r], fill_len_ref[r]), xl_ref.at[0], xs_ref,
                             sem, src_step=0)
        _drain(n_fill, xl_ref.at[0], xs_ref, sem)
        pend_ref[0] = 0

    src = xl_ref.at[i % 2]
    perm = _sort_matrix(lp_ref[...], rows * ROW_PACK, 0).astype(jnp.bfloat16)
    sorted_rows = jnp.dot(perm, xb_ref[...], preferred_element_type=jnp.float32).astype(jnp.bfloat16)
    src[:rows] = sorted_rows.reshape(rows, ROW_PACK, d)

    def run_args(e):
        r = i * n_experts + e
        return loff_ref[r], dst_ref[r], cnt_ref[r]

    _drain(pend_ref[0], src, xs_ref, sem)
    pend_ref[0] = _start_runs(n_experts, run_args, src, xs_ref, sem)

    @pl.when(i == pl.num_programs(0) - 1)
    def _():
        _drain(pend_ref[0], src, xs_ref, sem)


def _dispatch(xb, lp, cnt_flat, loff_flat, dst_flat, fill_at, fill_len, m_pad, local_pairs, *, tm, n_experts):
    n, d = xb.shape
    grid_spec = pltpu.PrefetchScalarGridSpec(
        num_scalar_prefetch=5,
        grid=(n // tm,),
        in_specs=[pl.BlockSpec((tm, d), lambda i, *_: (i, 0)), pl.BlockSpec((TOP_K, tm), lambda i, *_: (0, i))],
        out_specs=pl.BlockSpec(memory_space=pl.ANY),
        scratch_shapes=[pltpu.VMEM((2, local_pairs + RUN_CHUNK, ROW_PACK, d), jnp.bfloat16),
                        pltpu.SMEM((1,), jnp.int32), pltpu.SemaphoreType.DMA],
    )
    return pl.pallas_call(
        functools.partial(_dispatch_kernel, n_experts=n_experts),
        out_shape=jax.ShapeDtypeStruct((m_pad, ROW_PACK, d), jnp.bfloat16),
        grid_spec=grid_spec,
        compiler_params=pltpu.CompilerParams(dimension_semantics=("arbitrary",), vmem_limit_bytes=VMEM_LIMIT),
        name="dispatch",
    )(cnt_flat, loff_flat, dst_flat, fill_at, fill_len, xb, lp)


def _expert_kernel(te_ref, nu_ref, tj_ref, ilo_ref, ihi_ref, cnt_ref, loff_ref, pre_ref, bfill_at_ref, bfill_len_ref,
                   xs_ref, wgu_ref, bgu_ref, wd_ref, bd_ref, ys_ref, wgu_bf_ref, wd_bf_ref, yb_ref, pend_ref, sem,
                   *, n_experts, blk_rows):
    j = pl.program_id(0)
    tm, _, d = xs_ref.shape

    @pl.when(j == 0)
    def _():
        for s in range(2):
            yb_ref[s, tm:] = jnp.zeros((RUN_CHUNK, ROW_PACK, d), yb_ref.dtype)
        n_blocks = ys_ref.shape[0] // blk_rows
        n_fill = _start_runs(n_blocks, lambda r: (tm, bfill_at_ref[r], bfill_len_ref[r]), yb_ref.at[0],
                             ys_ref, sem, src_step=0)
        _drain(n_fill, yb_ref.at[0], ys_ref, sem)
        pend_ref[0] = 0

    src = yb_ref.at[j % 2]

    @pl.when(j < nu_ref[0])
    def _():
        d_exp = wd_ref.shape[0]

        @pl.when(tj_ref[j] == 0)
        def _():
            wgu_bf_ref[...] = wgu_ref[...].astype(jnp.bfloat16)
            wd_bf_ref[...] = wd_ref[...].astype(jnp.bfloat16)

        x = xs_ref[...].reshape(tm * ROW_PACK, d)
        gu = jnp.dot(x, wgu_bf_ref[...], preferred_element_type=jnp.float32) + bgu_ref[...]
        g = jnp.minimum(gu[:, :d_exp], SWIGLU_LIMIT)
        lin = jnp.clip(gu[:, d_exp:], -SWIGLU_LIMIT, SWIGLU_LIMIT)
        act = g * _sigmoid(SWIGLU_ALPHA * g) * (lin + 1.0)
        y = jnp.dot(act.astype(jnp.bfloat16), wd_bf_ref[...], preferred_element_type=jnp.float32) + bd_ref[...]
        src[:tm] = y.astype(jnp.bfloat16).reshape(tm, ROW_PACK, d)

    _drain(pend_ref[0], src, ys_ref, sem)
    pend_ref[0] = 0

    @pl.when(j < nu_ref[0])
    def _():
        e = te_ref[j]
        base = tj_ref[j] * tm
        i0 = ilo_ref[j]

        def run_args(r):
            it = i0 + r
            a = pre_ref[it * n_experts + e] - base
            b = a + cnt_ref[it * n_experts + e]
            ac = jnp.maximum(a, 0)
            return ac, it * blk_rows + loff_ref[it * n_experts + e] + (ac - a), jnp.minimum(b, tm) - ac

        pend_ref[0] = _start_runs(ihi_ref[j] - i0, run_args, src, ys_ref, sem)

    @pl.when(j == pl.num_programs(0) - 1)
    def _():
        _drain(pend_ref[0], src, ys_ref, sem)


def _experts(xs, sched, w_gate_up, b_gate_up, w_down, b_down, n_tok_tiles, blk_rows, *, tm):
    m_pad = xs.shape[0]
    e, d, d_gu = w_gate_up.shape
    d_exp = w_down.shape[1]
    bf16, f32 = jnp.bfloat16, jnp.float32
    row_map = lambda j, te, nu, *_: (jnp.minimum(j, nu[0] - 1), 0, 0)
    w_map = lambda j, te, *_: (te[j], 0, 0)
    grid_spec = pltpu.PrefetchScalarGridSpec(
        num_scalar_prefetch=len(sched),
        grid=(m_pad // tm,),
        in_specs=[pl.BlockSpec((tm, ROW_PACK, d), row_map),
                  pl.BlockSpec((None, d, d_gu), w_map), pl.BlockSpec((None, 1, d_gu), w_map),
                  pl.BlockSpec((None, d_exp, d), w_map), pl.BlockSpec((None, 1, d), w_map)],
        out_specs=pl.BlockSpec(memory_space=pl.ANY),
        scratch_shapes=[pltpu.VMEM((d, d_gu), bf16), pltpu.VMEM((d_exp, d), bf16),
                        pltpu.VMEM((2, tm + RUN_CHUNK, ROW_PACK, d), bf16),
                        pltpu.SMEM((1,), jnp.int32), pltpu.SemaphoreType.DMA],
    )
    return pl.pallas_call(
        functools.partial(_expert_kernel, n_experts=e, blk_rows=blk_rows),
        out_shape=jax.ShapeDtypeStruct((n_tok_tiles * blk_rows, ROW_PACK, d), bf16),
        grid_spec=grid_spec,
        compiler_params=pltpu.CompilerParams(dimension_semantics=("arbitrary",), vmem_limit_bytes=VMEM_LIMIT),
        name="experts",
    )(*sched, xs, w_gate_up.astype(f32), b_gate_up.reshape(e, 1, d_gu).astype(f32),
      w_down.astype(f32), b_down.reshape(e, 1, d).astype(f32))


def _combine_kernel(h_ref, ys_ref, lpt_ref, pt_ref, g_ref, out_ref, *, tm, blk_rows):
    d = h_ref.shape[1]
    pairs = blk_rows - RUN_CHUNK
    rows = pairs * ROW_PACK
    for s in range(h_ref.shape[0] // tm):
        tok = slice(s * tm, (s + 1) * tm)
        y = ys_ref[s * blk_rows:s * blk_rows + pairs].reshape(rows, d)
        col = lax.broadcasted_iota(jnp.int32, (tm, rows), 1)
        w = jnp.zeros((tm, rows), jnp.float32)
        for k in range(TOP_K):
            w = jnp.where(col == lpt_ref[tok, k:k + 1], pt_ref[tok, k:k + 1], w)
        w_hi = w.astype(jnp.bfloat16)
        w_lo = (w - w_hi.astype(jnp.float32)).astype(jnp.bfloat16)
        acc = (h_ref[tok, :] + jnp.dot(w_hi, y, preferred_element_type=jnp.float32)
               + jnp.dot(w_lo, y, preferred_element_type=jnp.float32))
        out_ref[tok, :] = acc * lax.rsqrt(jnp.mean(acc * acc, axis=-1, keepdims=True) + RMS_EPS) * g_ref[...]


def _combine(h1, ys, lpt, prob_t, norm_g, blk_rows, *, tm):
    n, d = h1.shape
    f32 = jnp.float32
    sub = TILES_PER_STEP if (n // tm) % TILES_PER_STEP == 0 else 1
    blk = sub * tm
    tspec = pl.BlockSpec((blk, TOP_K), lambda i: (i, 0))
    return pl.pallas_call(
        functools.partial(_combine_kernel, tm=tm, blk_rows=blk_rows),
        out_shape=jax.ShapeDtypeStruct((n, d), f32),
        grid=(n // blk,),
        in_specs=[pl.BlockSpec((blk, d), lambda i: (i, 0)),
                  pl.BlockSpec((sub * blk_rows, ROW_PACK, d), lambda i: (i, 0, 0)),
                  tspec, tspec, _const_spec((1, d))],
        out_specs=pl.BlockSpec((blk, d), lambda i: (i, 0)),
        compiler_params=pltpu.CompilerParams(dimension_semantics=("arbitrary",), vmem_limit_bytes=VMEM_LIMIT),
        name="combine",
    )(h1, ys, lpt, prob_t, norm_g.reshape(1, d).astype(f32))


def _tile_rows(n, target):
    t = min(target, n)
    while n % t:
        t //= 2
    return t


def _moe_schedule(cnt, tm_tok, tm_e, blk_rows):
    n_t, e = cnt.shape
    i32 = jnp.int32
    total = jnp.sum(cnt, axis=0)
    tiles_per_e = (total + RUN_CHUNK + tm_e - 1) // tm_e
    tile_end = jnp.cumsum(tiles_per_e)
    tile_begin = tile_end - tiles_per_e
    n_used = tile_end[-1:]
    n_tiles = ((TOP_K * tm_tok + e) * n_t) // (ROW_PACK * tm_e) + 2 * e + 1
    tile_id = jnp.minimum(jnp.arange(n_tiles, dtype=i32), n_used[0] - 1)
    tile_expert = jnp.sum(tile_id[:, None] >= tile_end[None, :], axis=1).astype(i32)
    tile_j = tile_id - tile_begin[tile_expert]
    pre = jnp.cumsum(cnt, axis=0) - cnt
    loff = jnp.cumsum(cnt, axis=1) - cnt
    dst = tile_begin[None, :] * tm_e + pre
    pre_t = pre[:, tile_expert].T
    end_t = pre_t + cnt[:, tile_expert].T
    ilo = jnp.sum(end_t <= (tile_j * tm_e)[:, None], axis=1).astype(i32)
    ihi = jnp.sum(pre_t < ((tile_j + 1) * tm_e)[:, None], axis=1).astype(i32)
    gap = tiles_per_e * tm_e - total
    fill_len = (gap + RUN_CHUNK - 1) // RUN_CHUNK * RUN_CHUNK
    fill_at = jnp.concatenate([tile_end * tm_e - fill_len, n_used * tm_e])
    fill_len = jnp.concatenate([fill_len, (n_tiles - n_used) * tm_e])
    bfill_len = (blk_rows - jnp.sum(cnt, axis=1) + RUN_CHUNK - 1) // RUN_CHUNK * RUN_CHUNK
    bfill_at = (jnp.arange(n_t, dtype=i32) + 1) * blk_rows - bfill_len
    flat = lambda a: a.reshape(-1).astype(i32)
    return dict(n_tiles=n_tiles, tile_expert=tile_expert, n_used=n_used.astype(i32), tile_j=tile_j.astype(i32),
                ilo=ilo, ihi=ihi, cnt=flat(cnt), loff=flat(loff), pre=flat(pre), dst=flat(dst),
                fill_at=flat(fill_at), fill_len=flat(fill_len), bfill_at=flat(bfill_at), bfill_len=flat(bfill_len))


def _moe_and_final_norm(h1, norm_ffn_g, router_w, router_b, w_gate_up, b_gate_up, w_down, b_down, norm_final_g):
    n, d = h1.shape
    e = router_w.shape[-1]
    tm_tok = _tile_rows(n, 256)
    tm_e = _tile_rows(n, 512) // ROW_PACK
    local_rows = -(-(TOP_K * tm_tok + e) // (ROW_PACK * RUN_CHUNK)) * (ROW_PACK * RUN_CHUNK)
    blk_rows = local_rows // ROW_PACK + RUN_CHUNK
    xb, lp, prob, cnt3 = _router(h1, norm_ffn_g, router_w, router_b, tm=tm_tok)
    s = _moe_schedule(cnt3[:, :, 0], tm_tok, tm_e, blk_rows)
    xs = _dispatch(xb, lp, s["cnt"], s["loff"], s["dst"], s["fill_at"], s["fill_len"], s["n_tiles"] * tm_e,
                   local_rows // ROW_PACK, tm=tm_tok, n_experts=e)
    sched = (s["tile_expert"], s["n_used"], s["tile_j"], s["ilo"], s["ihi"], s["cnt"], s["loff"], s["pre"],
             s["bfill_at"], s["bfill_len"])
    ys = _experts(xs, sched, w_gate_up, b_gate_up, w_down, b_down, n // tm_tok, blk_rows, tm=tm_e)
    return _combine(h1, ys, lp.T, prob.T, norm_final_g, blk_rows, tm=tm_tok)


def kernel(x, meta_tokens, norm_mix_g, w_in, ssm_a_re, ssm_a_im, ssm_log_dt, ssm_b_re, ssm_b_im, ssm_c_re, ssm_c_im, ssm_d, w_s5_lin, w_s5_gate, conv_dw_w, conv_dw_b, conv_ln_g, conv_ln_b, w_conv_out, w_out, norm_ffn_g, router_w, router_b, w_gate_up, b_gate_up, w_down, b_down, norm_final_g):
    assert norm_mix_g.shape[0] == 1, "single-layer block"
    bsz, seq, d_model = x.shape
    h1 = _mixer(x, meta_tokens, norm_mix_g[0], w_in[0], ssm_a_re[0], ssm_a_im[0], ssm_log_dt[0], ssm_b_re[0],
                ssm_b_im[0], ssm_c_re[0], ssm_c_im[0], ssm_d[0], w_s5_lin[0], w_s5_gate[0], conv_dw_w[0],
                conv_dw_b[0], conv_ln_g[0], conv_ln_b[0], w_conv_out[0], w_out[0], t_chunk=MIXER_T_CHUNK)
    out = _moe_and_final_norm(h1.reshape(bsz * seq, d_model), norm_ffn_g[0], router_w[0], router_b[0],
                              w_gate_up[0], b_gate_up[0], w_down[0], b_down[0], norm_final_g)
    return out.reshape(bsz, seq, d_model).astype(x.dtype)
```

```python
import functools
import math

import jax
import jax.numpy as jnp
from jax import lax
from jax.experimental import pallas as pl
from jax.experimental.pallas import tpu as pltpu

RMS_EPS = 1e-6
LN_EPS = 1e-5
TOP_K = 4
SWIGLU_LIMIT = 7.0
SWIGLU_ALPHA = 1.702

LANES = 128
SUBLANES = 8
VMEM_LIMIT = 56 * 1024 * 1024
SCAN_CHAINS = 8
CONV_OUT_TILE = 8
MIXER_T_CHUNK = 64


def _sigmoid(v):
    return 0.5 * jnp.tanh(0.5 * v) + 0.5


def _const_spec(shape):
    nd = len(shape)
    return pl.BlockSpec(shape, lambda *_: (0,) * nd)


def _mixer_kernel(x_ref, meta_ref, g_ref, win_ref, ar_ref, ai_ref, bd_ref, ct_ref, dskip_ref,
                  wlin_ref, wgate_ref, dww_ref, dwb_ref, lng_ref, lnb_ref, wco_ref, wout_ref,
                  out_ref,
                  xs_ref, st_ref, state_ref, cbuf_ref, conv_ref,
                  *, t_chunk, d_ssm, d_conv, conv_w, sb_lanes):
    i = pl.program_id(0)
    bsz = SUBLANES
    rows = t_chunk * bsz
    d_model = xs_ref.shape[1]
    n_sb = d_ssm // LANES
    hist = (conv_w - 1) * bsz

    @pl.when(i == 0)
    def _():
        state_ref[...] = jnp.zeros_like(state_ref)
        cbuf_ref[...] = jnp.zeros_like(cbuf_ref)
        for t in range(t_chunk):
            xs_ref[t * bsz:(t + 1) * bsz, :] = jnp.broadcast_to(meta_ref[t:t + 1, :], (bsz, d_model))

    @pl.when(i > 0)
    def _():
        for t in range(t_chunk):
            xs_ref[t * bsz:(t + 1) * bsz, :] = x_ref[:, pl.ds(t, 1), :].reshape(bsz, d_model)

    xs = xs_ref[...]
    hn = xs * lax.rsqrt(jnp.mean(xs * xs, axis=-1, keepdims=True) + RMS_EPS) * g_ref[...]
    hn_bf = hn.astype(jnp.bfloat16)
    s1 = d_ssm
    s2 = s1 + 2 * d_conv
    in_proj = lambda lo, hi: jnp.dot(hn_bf, win_ref[:, lo:hi], preferred_element_type=jnp.float32)

    zc = in_proj(s1, s2)
    cbuf_ref[hist:hist + rows, :] = zc[:, :d_conv] * _sigmoid(zc[:, d_conv:])

    u = in_proj(0, s1)
    u_bf = u.astype(jnp.bfloat16)
    for sb in range(n_sb):
        st_ref[:, sb * 2 * sb_lanes:(sb + 1) * 2 * sb_lanes] = jnp.dot(
            u_bf[:, sb * LANES:(sb + 1) * LANES], bd_ref[sb], preferred_element_type=jnp.float32)

    for lb in range(d_conv // LANES):
        ls = slice(lb * LANES, (lb + 1) * LANES)
        wk = [jnp.broadcast_to(dww_ref[k:k + 1, ls], (bsz, LANES)) for k in range(conv_w)]
        for t0 in range(0, t_chunk, CONV_OUT_TILE):
            acc = [None] * CONV_OUT_TILE
            for tt in range(t0, t0 + CONV_OUT_TILE + conv_w - 1):
                v = cbuf_ref[tt * bsz:(tt + 1) * bsz, ls]
                for o in range(CONV_OUT_TILE):
                    k = tt - (t0 + o)
                    if 0 <= k < conv_w:
                        acc[o] = wk[k] * v if acc[o] is None else acc[o] + wk[k] * v
            for o in range(CONV_OUT_TILE):
                conv_ref[(t0 + o) * bsz:(t0 + o + 1) * bsz, ls] = acc[o]
    cbuf_ref[0:hist, :] = cbuf_ref[rows:rows + hist, :]

    zg = in_proj(s2, win_ref.shape[1])

    n_lb = sb_lanes // LANES
    blocks = []
    for sb in range(n_sb):
        base = sb * 2 * sb_lanes
        for j in range(n_lb):
            blocks.append((slice(base + j * LANES, base + (j + 1) * LANES),
                           slice(base + sb_lanes + j * LANES, base + sb_lanes + (j + 1) * LANES),
                           slice(sb * sb_lanes + j * LANES, sb * sb_lanes + (j + 1) * LANES)))
    for g0 in range(0, len(blocks), SCAN_CHAINS):
        grp = blocks[g0:g0 + SCAN_CHAINS]
        ar = [jnp.broadcast_to(ar_ref[:, la], (bsz, LANES)) for _, _, la in grp]
        ai = [jnp.broadcast_to(ai_ref[:, la], (bsz, LANES)) for _, _, la in grp]
        sre = [state_ref[:, lre] for lre, _, _ in grp]
        sim = [state_ref[:, lim] for _, lim, _ in grp]
        for t in range(t_chunk):
            r = slice(t * bsz, (t + 1) * bsz)
            for c, (lre, lim, _) in enumerate(grp):
                nre = ar[c] * sre[c] - ai[c] * sim[c] + st_ref[r, lre]
                nim = ar[c] * sim[c] + ai[c] * sre[c] + st_ref[r, lim]
                st_ref[r, lre] = nre
                st_ref[r, lim] = nim
                sre[c], sim[c] = nre, nim
        for c, (lre, lim, _) in enumerate(grp):
            state_ref[:, lre] = sre[c]
            state_ref[:, lim] = sim[c]
    ys = []
    for sb in range(n_sb):
        ys.append(jnp.dot(st_ref[:, sb * 2 * sb_lanes:(sb + 1) * 2 * sb_lanes].astype(jnp.bfloat16),
                          ct_ref[sb], preferred_element_type=jnp.float32))
    y = jnp.concatenate(ys, axis=-1) if n_sb > 1 else ys[0]
    y = y + dskip_ref[...] * u
    ya = 0.5 * y * (1.0 + lax.erf(y * (1.0 / math.sqrt(2.0))))
    ya_bf = ya.astype(jnp.bfloat16)
    ya2 = (jnp.dot(ya_bf, wlin_ref[...], preferred_element_type=jnp.float32)
           * _sigmoid(jnp.dot(ya_bf, wgate_ref[...], preferred_element_type=jnp.float32)))

    c = conv_ref[...] + dwb_ref[...]
    mu = jnp.mean(c, axis=-1, keepdims=True)
    cc = c - mu
    var = jnp.mean(cc * cc, axis=-1, keepdims=True)
    c = cc * lax.rsqrt(var + LN_EPS) * lng_ref[...] + lnb_ref[...]
    c = c * _sigmoid(c)
    yb = jnp.dot(c.astype(jnp.bfloat16), wco_ref[...], preferred_element_type=jnp.float32)

    merged = _sigmoid(zg[:, :d_model]) * ya2 + _sigmoid(zg[:, d_model:]) * yb
    h1 = xs + jnp.dot(merged.astype(jnp.bfloat16), wout_ref[...], preferred_element_type=jnp.float32)
    xs_ref[...] = h1
    for t in range(t_chunk):
        out_ref[:, pl.ds(t, 1), :] = xs_ref[t * bsz:(t + 1) * bsz, :].reshape(bsz, 1, d_model)


def _s5_discretize(a_re, a_im, log_dt, b_re, b_im, c_re, c_im):
    f32 = jnp.float32
    g, p = a_re.shape
    h = b_re.shape[-1]
    gps = LANES // h
    n_sb = g // gps
    dt = jnp.exp(log_dt.astype(f32))[:, None]
    decay = jnp.exp(a_re * dt)
    abar_re = decay * jnp.cos(a_im * dt)
    abar_im = decay * jnp.sin(a_im * dt)
    den = a_re * a_re + a_im * a_im
    nr = abar_re - 1.0
    coef_re = (nr * a_re + abar_im * a_im) / den
    coef_im = (abar_im * a_re - nr * a_im) / den
    bd_re = coef_re[..., None] * b_re - coef_im[..., None] * b_im
    bd_im = coef_re[..., None] * b_im + coef_im[..., None] * b_re
    eye = jnp.eye(gps, dtype=f32)

    def blockdiag_in(w):
        w = w.reshape(n_sb, gps, p, h)
        return jnp.einsum('sgph,gk->sghkp', w, eye).reshape(n_sb, gps * h, gps * p)

    def blockdiag_out(w):
        w = w.reshape(n_sb, gps, h, p)
        return jnp.einsum('sghp,gk->skpgh', w, eye).reshape(n_sb, gps * p, gps * h)

    bd = jnp.concatenate([blockdiag_in(bd_re), blockdiag_in(bd_im)], axis=-1)
    ct = jnp.concatenate([blockdiag_out(c_re), -blockdiag_out(c_im)], axis=1)
    return (abar_re.reshape(1, g * p), abar_im.reshape(1, g * p),
            bd.astype(jnp.bfloat16), ct.astype(jnp.bfloat16), gps * p)


def _mixer(x, meta_tokens, norm_g, w_in, a_re, a_im, log_dt, b_re, b_im, c_re, c_im, d_skip,
           w_s5_lin, w_s5_gate, dw_w, dw_b, ln_g, ln_b, w_conv_out, w_out, *, t_chunk):
    bsz, seq, d_model = x.shape
    n_meta = meta_tokens.shape[0]
    g, p = a_re.shape
    d_ssm = g * b_re.shape[-1]
    conv_w, _, d_conv = dw_w.shape
    assert bsz == SUBLANES and seq % t_chunk == 0 and t_chunk >= n_meta and t_chunk % SUBLANES == 0
    assert d_ssm % LANES == 0 and d_conv % LANES == 0
    f32, bf16 = jnp.float32, jnp.bfloat16
    ar, ai, bd, ct, sb_lanes = _s5_discretize(a_re.astype(f32), a_im.astype(f32), log_dt, b_re.astype(f32),
                                              b_im.astype(f32), c_re.astype(f32), c_im.astype(f32))
    meta_chunk = jnp.concatenate([jnp.zeros((t_chunk - n_meta, d_model), f32), meta_tokens.astype(f32)], axis=0)
    rows = t_chunk * bsz
    n_state = g * p
    consts = [
        meta_chunk, norm_g.reshape(1, d_model).astype(f32), w_in.astype(bf16), ar, ai, bd, ct,
        d_skip.reshape(1, d_ssm).astype(f32), w_s5_lin.astype(bf16), w_s5_gate.astype(bf16),
        dw_w.reshape(conv_w, d_conv).astype(f32), dw_b.reshape(1, d_conv).astype(f32),
        ln_g.reshape(1, d_conv).astype(f32), ln_b.reshape(1, d_conv).astype(f32),
        w_conv_out.astype(bf16), w_out.astype(bf16),
    ]
    x_spec = pl.BlockSpec((bsz, t_chunk, d_model), lambda i: (0, jnp.maximum(i - 1, 0), 0))
    kern = functools.partial(_mixer_kernel, t_chunk=t_chunk, d_ssm=d_ssm, d_conv=d_conv, conv_w=conv_w,
                             sb_lanes=sb_lanes)
    return pl.pallas_call(
        kern,
        out_shape=jax.ShapeDtypeStruct((bsz, seq, d_model), f32),
        grid=(seq // t_chunk + 1,),
        in_specs=[x_spec] + [_const_spec(c.shape) for c in consts],
        out_specs=x_spec,
        scratch_shapes=[
            pltpu.VMEM((rows, d_model), f32),
            pltpu.VMEM((rows, 2 * n_state), f32),
            pltpu.VMEM((bsz, 2 * n_state), f32),
            pltpu.VMEM(((conv_w - 1) * bsz + rows, d_conv), f32),
            pltpu.VMEM((rows, d_conv), f32),
        ],
        compiler_params=pltpu.CompilerParams(dimension_semantics=("arbitrary",), vmem_limit_bytes=VMEM_LIMIT),
        name="mixer",
    )(x.astype(f32), *consts)


ROW_PACK = 2
RUN_CHUNK = 32
WAIT_GROUP = 8
TILES_PER_STEP = 4


def _router_kernel(h_ref, g_ref, rwh_ref, rwl_ref, rb_ref, xb_ref, lp_ref, prob_ref, cnt_ref, *, n_experts, tm):
    for s in range(h_ref.shape[0] // tm):
        rows = slice(s * tm, (s + 1) * tm)
        hn_hi, lps, probs, pairs = _route_tile(h_ref[rows, :], g_ref[...], rwh_ref[...], rwl_ref[...], rb_ref[...],
                                               n_experts)
        xb_ref[rows, :] = hn_hi
        for k in range(TOP_K):
            prob_ref[k:k + 1, rows] = probs[k]
            lp_ref[k:k + 1, rows] = lps[k]
        cnt_ref[s] = pairs


def _route_tile(h, g, rwh, rwl, rb, n_experts):
    tm = h.shape[0]
    hn = h * lax.rsqrt(jnp.mean(h * h, axis=-1, keepdims=True) + RMS_EPS) * g
    hn_hi = hn.astype(jnp.bfloat16)
    hn_lo = (hn - hn_hi.astype(jnp.float32)).astype(jnp.bfloat16)
    nt = (((1,), (1,)), ((), ()))
    dot = lambda a, b: lax.dot_general(a, b, nt, preferred_element_type=jnp.float32)
    lg = dot(rwh, hn_hi) + dot(rwh, hn_lo) + dot(rwl, hn_hi) + rb
    eidx = lax.broadcasted_iota(jnp.int32, (n_experts, tm), 0)
    work = lg
    vals, hots = [], []
    for _ in range(TOP_K):
        m = jnp.max(work, axis=0, keepdims=True)
        sel = jnp.min(jnp.where(work == m, eidx, n_experts), axis=0, keepdims=True)
        hot = eidx == sel
        vals.append(m)
        hots.append(hot)
        work = jnp.where(hot, -jnp.inf, work)
    exps = [jnp.exp(v - vals[0]) for v in vals]
    tot = exps[0]
    for e in exps[1:]:
        tot = tot + e
    inv = 1.0 / tot
    oh = hots[0].astype(jnp.float32)
    for hot in hots[1:]:
        oh = oh + hot.astype(jnp.float32)
    upper = (lax.broadcasted_iota(jnp.int32, (tm, tm), 0) < lax.broadcasted_iota(jnp.int32, (tm, tm), 1))
    cum = jnp.dot(oh.astype(jnp.bfloat16), upper.astype(jnp.bfloat16), preferred_element_type=jnp.float32)
    cnt = jnp.sum(oh, axis=1, keepdims=True)
    lower = (lax.broadcasted_iota(jnp.int32, (n_experts, n_experts), 1)
             < lax.broadcasted_iota(jnp.int32, (n_experts, n_experts), 0))
    cnt_b = jnp.broadcast_to(cnt, (n_experts, LANES))
    pairs = jnp.floor((cnt_b + 1.0) * 0.5)
    cnt_b = 2.0 * pairs
    c_hi = jnp.floor(cnt_b * (1.0 / 16.0))
    c_lo = cnt_b - 16.0 * c_hi
    ldot = lambda v: jnp.dot(lower.astype(jnp.bfloat16), v.astype(jnp.bfloat16), preferred_element_type=jnp.float32)
    loff = 16.0 * ldot(c_hi) + ldot(c_lo)
    base = cum + loff[:, 0:1]
    probs = [ex * inv for ex in exps]
    lps = [jnp.sum(jnp.where(hot, base, 0.0), axis=0, keepdims=True).astype(jnp.int32) for hot in hots]
    return hn_hi, lps, probs, pairs.astype(jnp.int32)


def _router(h1, norm_g, router_w, router_b, *, tm):
    n, d = h1.shape
    e = router_w.shape[-1]
    f32 = jnp.float32
    rwt = router_w.astype(f32).T
    rwh = rwt.astype(jnp.bfloat16)
    rwl = (rwt - rwh.astype(f32)).astype(jnp.bfloat16)
    kn = lambda dt: jax.ShapeDtypeStruct((TOP_K, n), dt)
    sub = 1
    blk = sub * tm
    kspec = pl.BlockSpec((TOP_K, blk), lambda i: (0, i))
    return pl.pallas_call(
        functools.partial(_router_kernel, n_experts=e, tm=tm),
        out_shape=(jax.ShapeDtypeStruct((n, d), jnp.bfloat16), kn(jnp.int32), kn(f32),
                   jax.ShapeDtypeStruct((n // tm, e, LANES), jnp.int32)),
        grid=(n // blk,),
        in_specs=[pl.BlockSpec((blk, d), lambda i: (i, 0)), _const_spec((1, d)), _const_spec((e, d)),
                  _const_spec((e, d)), _const_spec((e, 1))],
        out_specs=(pl.BlockSpec((blk, d), lambda i: (i, 0)), kspec, kspec,
                   pl.BlockSpec((sub, e, LANES), lambda i: (i, 0, 0))),
        compiler_params=pltpu.CompilerParams(dimension_semantics=("arbitrary",), vmem_limit_bytes=VMEM_LIMIT),
        name="router",
    )(h1, norm_g.reshape(1, d).astype(f32), rwh, rwl, router_b.reshape(e, 1).astype(f32))


def _sort_matrix(lp, n_rows, axis):
    if axis == 0:
        shape = (n_rows, lp.shape[1])
        hit = lambda k: lax.broadcasted_iota(jnp.int32, shape, 0) == lp[k:k + 1, :]
    else:
        shape = (lp.shape[0], n_rows)
        hit = lambda k: lax.broadcasted_iota(jnp.int32, shape, 1) == lp[:, k:k + 1]
    m = hit(0)
    for k in range(1, TOP_K):
        m = m | hit(k)
    return m


def _start_runs(n_runs, run_args, src_ref, dst_ref, sem, src_step=RUN_CHUNK):
    def one_run(r, started):
        s, d, ln = run_args(r)
        n_ch = (jnp.maximum(ln, 0) + (RUN_CHUNK - 1)) // RUN_CHUNK

        def one_chunk(q, c):
            pltpu.make_async_copy(src_ref.at[pl.ds(s + q * src_step, RUN_CHUNK)],
                                  dst_ref.at[pl.ds(d + q * RUN_CHUNK, RUN_CHUNK)], sem).start()
            return c

        lax.fori_loop(0, n_ch, one_chunk, 0)
        return started + n_ch

    return lax.fori_loop(0, n_runs, one_run, jnp.int32(0))


def _drain(count, src_ref, dst_ref, sem):
    def wait_chunks(n_chunks):
        def body(q, c):
            pltpu.make_async_copy(src_ref.at[pl.ds(0, n_chunks * RUN_CHUNK)],
                                  dst_ref.at[pl.ds(0, n_chunks * RUN_CHUNK)], sem).wait()
            return c
        return body

    groups = count // WAIT_GROUP
    lax.fori_loop(0, groups, wait_chunks(WAIT_GROUP), 0)
    lax.fori_loop(0, count - groups * WAIT_GROUP, wait_chunks(1), 0)


def _dispatch_kernel(cnt_ref, loff_ref, dst_ref, fill_at_ref, fill_len_ref, xb_ref, lp_ref, xs_ref,
                     xl_ref, pend_ref, sem, *, n_experts):
    i = pl.program_id(0)
    rows = xl_ref.shape[1] - RUN_CHUNK
    d = xl_ref.shape[3]

    @pl.when(i == 0)
    def _():
        for s in range(2):
            xl_ref[s, rows:] = jnp.zeros((RUN_CHUNK, ROW_PACK, d), xl_ref.dtype)
        n_fill = _start_runs(n_experts + 1, lambda r: (rows, fill_at_ref[r], fill_len_ref[r]), xl_ref.at[0], xs_ref,
                             sem, src_step=0)
        _drain(n_fill, xl_ref.at[0], xs_ref, sem)
        pend_ref[0] = 0

    src = xl_ref.at[i % 2]
    perm = _sort_matrix(lp_ref[...], rows * ROW_PACK, 0).astype(jnp.bfloat16)
    sorted_rows = jnp.dot(perm, xb_ref[...], preferred_element_type=jnp.float32).astype(jnp.bfloat16)
    src[:rows] = sorted_rows.reshape(rows, ROW_PACK, d)

    def run_args(e):
        r = i * n_experts + e
        return loff_ref[r], dst_ref[r], cnt_ref[r]

    _drain(pend_ref[0], src, xs_ref, sem)
    pend_ref[0] = _start_runs(n_experts, run_args, src, xs_ref, sem)

    @pl.when(i == pl.num_programs(0) - 1)
    def _():
        _drain(pend_ref[0], src, xs_ref, sem)


def _dispatch(xb, lp, cnt_flat, loff_flat, dst_flat, fill_at, fill_len, m_pad, local_pairs, *, tm, n_experts):
    n, d = xb.shape
    grid_spec = pltpu.PrefetchScalarGridSpec(
        num_scalar_prefetch=5,
        grid=(n // tm,),
        in_specs=[pl.BlockSpec((tm, d), lambda i, *_: (i, 0)), pl.BlockSpec((TOP_K, tm), lambda i, *_: (0, i))],
        out_specs=pl.BlockSpec(memory_space=pl.ANY),
        scratch_shapes=[pltpu.VMEM((2, local_pairs + RUN_CHUNK, ROW_PACK, d), jnp.bfloat16),
                        pltpu.SMEM((1,), jnp.int32), pltpu.SemaphoreType.DMA],
    )
    return pl.pallas_call(
        functools.partial(_dispatch_kernel, n_experts=n_experts),
        out_shape=jax.ShapeDtypeStruct((m_pad, ROW_PACK, d), jnp.bfloat16),
        grid_spec=grid_spec,
        compiler_params=pltpu.CompilerParams(dimension_semantics=("arbitrary",), vmem_limit_bytes=VMEM_LIMIT),
        name="dispatch",
    )(cnt_flat, loff_flat, dst_flat, fill_at, fill_len, xb, lp)


def _expert_kernel(te_ref, nu_ref, tj_ref, ilo_ref, ihi_ref, cnt_ref, loff_ref, pre_ref, bfill_at_ref, bfill_len_ref,
                   xs_ref, wgu_ref, bgu_ref, wd_ref, bd_ref, ys_ref, wgu_bf_ref, wd_bf_ref, yb_ref, pend_ref, sem,
                   *, n_experts, blk_rows):
    j = pl.program_id(0)
    tm, _, d = xs_ref.shape

    @pl.when(j == 0)
    def _():
        for s in range(2):
            yb_ref[s, tm:] = jnp.zeros((RUN_CHUNK, ROW_PACK, d), yb_ref.dtype)
        n_blocks = ys_ref.shape[0] // blk_rows
        n_fill = _start_runs(n_blocks, lambda r: (tm, bfill_at_ref[r], bfill_len_ref[r]), yb_ref.at[0],
                             ys_ref, sem, src_step=0)
        _drain(n_fill, yb_ref.at[0], ys_ref, sem)
        pend_ref[0] = 0

    src = yb_ref.at[j % 2]

    @pl.when(j < nu_ref[0])
    def _():
        d_exp = wd_ref.shape[0]

        @pl.when(tj_ref[j] == 0)
        def _():
            wgu_bf_ref[...] = wgu_ref[...].astype(jnp.bfloat16)
            wd_bf_ref[...] = wd_ref[...].astype(jnp.bfloat16)

        x = xs_ref[...].reshape(tm * ROW_PACK, d)
        gu = jnp.dot(x, wgu_bf_ref[...], preferred_element_type=jnp.float32) + bgu_ref[...]
        g = jnp.minimum(gu[:, :d_exp], SWIGLU_LIMIT)
        lin = jnp.clip(gu[:, d_exp:], -SWIGLU_LIMIT, SWIGLU_LIMIT)
        act = g * _sigmoid(SWIGLU_ALPHA * g) * (lin + 1.0)
        y = jnp.dot(act.astype(jnp.bfloat16), wd_bf_ref[...], preferred_element_type=jnp.float32) + bd_ref[...]
        src[:tm] = y.astype(jnp.bfloat16).reshape(tm, ROW_PACK, d)

    _drain(pend_ref[0], src, ys_ref, sem)
    pend_ref[0] = 0

    @pl.when(j < nu_ref[0])
    def _():
        e = te_ref[j]
        base = tj_ref[j] * tm
        i0 = ilo_ref[j]

        def run_args(r):
            it = i0 + r
            a = pre_ref[it * n_experts + e] - base
            b = a + cnt_ref[it * n_experts + e]
            ac = jnp.maximum(a, 0)
            return ac, it * blk_rows + loff_ref[it * n_experts + e] + (ac - a), jnp.minimum(b, tm) - ac

        pend_ref[0] = _start_runs(ihi_ref[j] - i0, run_args, src, ys_ref, sem)

    @pl.when(j == pl.num_programs(0) - 1)
    def _():
        _drain(pend_ref[0], src, ys_ref, sem)


def _experts(xs, sched, w_gate_up, b_gate_up, w_down, b_down, n_tok_tiles, blk_rows, *, tm):
    m_pad = xs.shape[0]
    e, d, d_gu = w_gate_up.shape
    d_exp = w_down.shape[1]
    bf16, f32 = jnp.bfloat16, jnp.float32
    row_map = lambda j, te, nu, *_: (jnp.minimum(j, nu[0] - 1), 0, 0)
    w_map = lambda j, te, *_: (te[j], 0, 0)
    grid_spec = pltpu.PrefetchScalarGridSpec(
        num_scalar_prefetch=len(sched),
        grid=(m_pad // tm,),
        in_specs=[pl.BlockSpec((tm, ROW_PACK, d), row_map),
                  pl.BlockSpec((None, d, d_gu), w_map), pl.BlockSpec((None, 1, d_gu), w_map),
                  pl.BlockSpec((None, d_exp, d), w_map), pl.BlockSpec((None, 1, d), w_map)],
        out_specs=pl.BlockSpec(memory_space=pl.ANY),
        scratch_shapes=[pltpu.VMEM((d, d_gu), bf16), pltpu.VMEM((d_exp, d), bf16),
                        pltpu.VMEM((2, tm + RUN_CHUNK, ROW_PACK, d), bf16),
                        pltpu.SMEM((1,), jnp.int32), pltpu.SemaphoreType.DMA],
    )
    return pl.pallas_call(
        functools.partial(_expert_kernel, n_experts=e, blk_rows=blk_rows),
        out_shape=jax.ShapeDtypeStruct((n_tok_tiles * blk_rows, ROW_PACK, d), bf16),
        grid_spec=grid_spec,
        compiler_params=pltpu.CompilerParams(dimension_semantics=("arbitrary",), vmem_limit_bytes=VMEM_LIMIT),
        name="experts",
    )(*sched, xs, w_gate_up.astype(f32), b_gate_up.reshape(e, 1, d_gu).astype(f32),
      w_down.astype(f32), b_down.reshape(e, 1, d).astype(f32))


def _combine_kernel(h_ref, ys_ref, lpt_ref, pt_ref, g_ref, out_ref, *, tm, blk_rows):
    d = h_ref.shape[1]
    pairs = blk_rows - RUN_CHUNK
    rows = pairs * ROW_PACK
    for s in range(h_ref.shape[0] // tm):
        tok = slice(s * tm, (s + 1) * tm)
        y = ys_ref[s * blk_rows:s * blk_rows + pairs].reshape(rows, d)
        col = lax.broadcasted_iota(jnp.int32, (tm, rows), 1)
        w = jnp.zeros((tm, rows), jnp.float32)
        for k in range(TOP_K):
            w = jnp.where(col == lpt_ref[tok, k:k + 1], pt_ref[tok, k:k + 1], w)
        w_hi = w.astype(jnp.bfloat16)
        w_lo = (w - w_hi.astype(jnp.float32)).astype(jnp.bfloat16)
        acc = (h_ref[tok, :] + jnp.dot(w_hi, y, preferred_element_type=jnp.float32)
               + jnp.dot(w_lo, y, preferred_element_type=jnp.float32))
        out_ref[tok, :] = acc * lax.rsqrt(jnp.mean(acc * acc, axis=-1, keepdims=True) + RMS_EPS) * g_ref[...]


def _combine(h1, ys, lpt, prob_t, norm_g, blk_rows, *, tm):
    n, d = h1.shape
    f32 = jnp.float32
    sub = TILES_PER_STEP if (n // tm) % TILES_PER_STEP == 0 else 1
    blk = sub * tm
    tspec = pl.BlockSpec((blk, TOP_K), lambda i: (i, 0))
    return pl.pallas_call(
        functools.partial(_combine_kernel, tm=tm, blk_rows=blk_rows),
        out_shape=jax.ShapeDtypeStruct((n, d), f32),
        grid=(n // blk,),
        in_specs=[pl.BlockSpec((blk, d), lambda i: (i, 0)),
                  pl.BlockSpec((sub * blk_rows, ROW_PACK, d), lambda i: (i, 0, 0)),
                  tspec, tspec, _const_spec((1, d))],
        out_specs=pl.BlockSpec((blk, d), lambda i: (i, 0)),
        compiler_params=pltpu.CompilerParams(dimension_semantics=("arbitrary",), vmem_limit_bytes=VMEM_LIMIT),
        name="combine",
    )(h1, ys, lpt, prob_t, norm_g.reshape(1, d).astype(f32))


def _tile_rows(n, target):
    t = min(target, n)
    while n % t:
        t //= 2
    return t


def _moe_schedule(cnt, tm_tok, tm_e, blk_rows):
    n_t, e = cnt.shape
    i32 = jnp.int32
    total = jnp.sum(cnt, axis=0)
    tiles_per_e = (total + RUN_CHUNK + tm_e - 1) // tm_e
    tile_end = jnp.cumsum(tiles_per_e)
    tile_begin = tile_end - tiles_per_e
    n_used = tile_end[-1:]
    n_tiles = ((TOP_K * tm_tok + e) * n_t) // (ROW_PACK * tm_e) + 2 * e + 1
    tile_id = jnp.minimum(jnp.arange(n_tiles, dtype=i32), n_used[0] - 1)
    tile_expert = jnp.sum(tile_id[:, None] >= tile_end[None, :], axis=1).astype(i32)
    tile_j = tile_id - tile_begin[tile_expert]
    pre = jnp.cumsum(cnt, axis=0) - cnt
    loff = jnp.cumsum(cnt, axis=1) - cnt
    dst = tile_begin[None, :] * tm_e + pre
    pre_t = pre[:, tile_expert].T
    end_t = pre_t + cnt[:, tile_expert].T
    ilo = jnp.sum(end_t <= (tile_j * tm_e)[:, None], axis=1).astype(i32)
    ihi = jnp.sum(pre_t < ((tile_j + 1) * tm_e)[:, None], axis=1).astype(i32)
    gap = tiles_per_e * tm_e - total
    fill_len = (gap + RUN_CHUNK - 1) // RUN_CHUNK * RUN_CHUNK
    fill_at = jnp.concatenate([tile_end * tm_e - fill_len, n_used * tm_e])
    fill_len = jnp.concatenate([fill_len, (n_tiles - n_used) * tm_e])
    bfill_len = (blk_rows - jnp.sum(cnt, axis=1) + RUN_CHUNK - 1) // RUN_CHUNK * RUN_CHUNK
    bfill_at = (jnp.arange(n_t, dtype=i32) + 1) * blk_rows - bfill_len
    flat = lambda a: a.reshape(-1).astype(i32)
    return dict(n_tiles=n_tiles, tile_expert=tile_expert, n_used=n_used.astype(i32), tile_j=tile_j.astype(i32),
                ilo=ilo, ihi=ihi, cnt=flat(cnt), loff=flat(loff), pre=flat(pre), dst=flat(dst),
                fill_at=flat(fill_at), fill_len=flat(fill_len), bfill_at=flat(bfill_at), bfill_len=flat(bfill_len))


def _moe_and_final_norm(h1, norm_ffn_g, router_w, router_b, w_gate_up, b_gate_up, w_down, b_down, norm_final_g):
    n, d = h1.shape
    e = router_w.shape[-1]
    tm_tok = _tile_rows(n, 256)
    tm_e = _tile_rows(n, 512) // ROW_PACK
    local_rows = -(-(TOP_K * tm_tok + e) // (ROW_PACK * RUN_CHUNK)) * (ROW_PACK * RUN_CHUNK)
    blk_rows = local_rows // ROW_PACK + RUN_CHUNK
    xb, lp, prob, cnt3 = _router(h1, norm_ffn_g, router_w, router_b, tm=tm_tok)
    s = _moe_schedule(cnt3[:, :, 0], tm_tok, tm_e, blk_rows)
    xs = _dispatch(xb, lp, s["cnt"], s["loff"], s["dst"], s["fill_at"], s["fill_len"], s["n_tiles"] * tm_e,
                   local_rows // ROW_PACK, tm=tm_tok, n_experts=e)
    sched = (s["tile_expert"], s["n_used"], s["tile_j"], s["ilo"], s["ihi"], s["cnt"], s["loff"], s["pre"],
             s["bfill_at"], s["bfill_len"])
    ys = _experts(xs, sched, w_gate_up, b_gate_up, w_down, b_down, n // tm_tok, blk_rows, tm=tm_e)
    return _combine(h1, ys, lp.T, prob.T, norm_final_g, blk_rows, tm=tm_tok)


def kernel(x, meta_tokens, norm_mix_g, w_in, ssm_a_re, ssm_a_im, ssm_log_dt, ssm_b_re, ssm_b_im, ssm_c_re, ssm_c_im, ssm_d, w_s5_lin, w_s5_gate, conv_dw_w, conv_dw_b, conv_ln_g, conv_ln_b, w_conv_out, w_out, norm_ffn_g, router_w, router_b, w_gate_up, b_gate_up, w_down, b_down, norm_final_g):
    assert norm_mix_g.shape[0] == 1, "single-layer block"
    bsz, seq, d_model = x.shape
    h1 = _mixer(x, meta_tokens, norm_mix_g[0], w_in[0], ssm_a_re[0], ssm_a_im[0], ssm_log_dt[0], ssm_b_re[0],
                ssm_b_im[0], ssm_c_re[0], ssm_c_im[0], ssm_d[0], w_s5_lin[0], w_s5_gate[0], conv_dw_w[0],
                conv_dw_b[0], conv_ln_g[0], conv_ln_b[0], w_conv_out[0], w_out[0], t_chunk=MIXER_T_CHUNK)
    out = _moe_and_final_norm(h1.reshape(bsz * seq, d_model), norm_ffn_g[0], router_w[0], router_b[0],
                              w_gate_up[0], b_gate_up[0], w_down[0], b_down[0], norm_final_g)
    return out.reshape(bsz, seq, d_model).astype(x.dtype)
```
